```python
import math
import functools
import jax
import jax.numpy as jnp
from jax import lax
import numpy as np


D_MODEL = 1024
BATCH = 16
SEQ = 2048
DEPTH = 1
DEC_BATCH = 128
DEC_SEQ = 4
PAST_LEN = 8192
PAGE_SIZE = 128

N_META = 16
H_A = D_MODEL // 256
DH_A = 64
DV_A = 2 * DH_A
ROT_A = DH_A // 4
H_B = D_MODEL // 256
Q_LORA = 3 * D_MODEL // 8
KV_LORA = D_MODEL // 4
NOPE_B = 64
ROPE_B = 32
DV_B = 128
MIX_WIDTH = H_A * DV_A + H_B * DV_B
D_FF = 4 * D_MODEL
ROPE_THETA = 500000.0
Q_BLOCK = 128
RMS_EPS = 1e-6
NEG_INF = -1e30
SCALE_A = DH_A ** -0.5
SCALE_B = (NOPE_B + ROPE_B) ** -0.5
IN_WIDTHS = (H_A * 2 * DH_A, H_A * 2 * DH_A, H_A * DV_A, Q_LORA, KV_LORA, ROPE_B)
IN_COLS = sum(IN_WIDTHS)
IN_SPLITS = tuple(int(v) for v in np.cumsum(IN_WIDTHS)[:-1])

kernel_name = 'hymba_diffattn_mla_decoder_step'


def _lambda_init(layer):
    return 0.8 - 0.6 * math.exp(-0.3 * layer)


def _rms(x, g):
    xf = x.astype(jnp.float32)
    y = xf * lax.rsqrt(jnp.mean(xf * xf, axis=-1, keepdims=True) + RMS_EPS)
    return (y * g.astype(jnp.float32)).astype(x.dtype)


def _rope(x, pos):
    half = x.shape[-1] // 2
    inv_freq = 1.0 / (ROPE_THETA ** (jnp.arange(half, dtype=jnp.float32) / half))
    ang = pos.astype(jnp.float32)[:, None] * inv_freq[None, :]
    shape = (pos.shape[0],) + (1,) * (x.ndim - 3) + (half,)
    cos = jnp.cos(ang).reshape(shape)
    sin = jnp.sin(ang).reshape(shape)
    xf = x.astype(jnp.float32)
    x1, x2 = xf[..., :half], xf[..., half:]
    return jnp.concatenate([x1 * cos - x2 * sin, x2 * cos + x1 * sin], axis=-1).astype(x.dtype)


def _partial_rope(x, pos):
    return jnp.concatenate([_rope(x[..., :ROT_A], pos), x[..., ROT_A:]], axis=-1)


def _project(h, pos, w_in, g_q_a, w_q_b, g_kv_a):
    b, t, _ = h.shape
    z = jnp.einsum('btd,dc->btc', h, w_in)
    qa, ka, va, q_lat, c_kv, k_r = jnp.split(z, IN_SPLITS, axis=-1)
    qa = _partial_rope(qa.reshape(b, t, H_A, 2, DH_A), pos)
    ka = _partial_rope(ka.reshape(b, t, H_A, 2, DH_A), pos)
    va = va.reshape(b, t, H_A, DV_A)
    qb = jnp.einsum('btr,rhe->bthe', _rms(q_lat, g_q_a), w_q_b)
    q_nope = qb[..., :NOPE_B]
    q_rope = _rope(qb[..., NOPE_B:], pos)
    c_kv = _rms(c_kv, g_kv_a)
    k_r = _rope(k_r, pos)
    return qa, ka, va, q_nope, q_rope, c_kv, k_r


def _prompt_attention(qa, ka, va, q_nope, q_rope, c_kv, k_r, w_kv_b):
    f32 = jnp.float32
    b, t = c_kv.shape[:2]
    k_nope = jnp.einsum('btc,che->bthe', c_kv, w_kv_b[..., :NOPE_B])
    v_b = jnp.einsum('btc,chv->bthv', c_kv, w_kv_b[..., NOPE_B:]).astype(f32)
    va32 = va.astype(f32)
    n_blk = -(-t // Q_BLOCK)
    pad = n_blk * Q_BLOCK - t

    def to_blocks(a):
        a = jnp.pad(a, [(0, 0), (0, pad)] + [(0, 0)] * (a.ndim - 2))
        return jnp.moveaxis(a.reshape((b, n_blk, Q_BLOCK) + a.shape[2:]), 1, 0)

    def from_blocks(a):
        a = jnp.moveaxis(a, 0, 1)
        return a.reshape((b, n_blk * Q_BLOCK) + a.shape[3:])[:, :t]

    k_pos = jnp.arange(t)

    def one_block(args):
        qa_blk, qn_blk, qr_blk, q_pos = args
        mask = k_pos[None, :] <= q_pos[:, None]
        s_a = jnp.einsum('bqhcd,bkhcd->bhcqk', qa_blk, ka).astype(f32) * SCALE_A
        p_a = jax.nn.softmax(jnp.where(mask, s_a, NEG_INF), axis=-1)
        o_a = jnp.einsum('bhcqk,bkhv->bqhcv', p_a, va32)
        s_b = (jnp.einsum('bqhe,bkhe->bhqk', qn_blk, k_nope)
               + jnp.einsum('bqhr,bkr->bhqk', qr_blk, k_r)).astype(f32) * SCALE_B
        p_b = jax.nn.softmax(jnp.where(mask, s_b, NEG_INF), axis=-1)
        o_b = jnp.einsum('bhqk,bkhv->bqhv', p_b, v_b)
        return o_a, o_b

    q_pos = jnp.arange(n_blk * Q_BLOCK).reshape(n_blk, Q_BLOCK)
    o_a, o_b = lax.map(one_block, (to_blocks(qa), to_blocks(q_nope), to_blocks(q_rope), q_pos))
    o_a = from_blocks(o_a)
    o_b = from_blocks(o_b)
    return o_a[..., 0, :], o_a[..., 1, :], o_b


def _online_softmax_update(carry, s, mask, pv):
    m, l, acc = carry
    if mask is not None:
        s = jnp.where(mask, s, NEG_INF)
    m_new = jnp.maximum(m, jnp.max(s, axis=-1))
    p = jnp.exp(s - m_new[..., None])
    if mask is not None:
        p = jnp.where(mask, p, 0.0)
    corr = jnp.exp(m - m_new)
    return m_new, l * corr + jnp.sum(p, axis=-1), acc * corr[..., None] + pv(p)


def _sample_attention(qa, ka, va, q_nope, q_rope, c_kv, k_r, w_kv_b,
                      cache_k, cache_v, cache_c, cache_kr, page_table, layer):
    f32 = jnp.float32
    b, t = c_kv.shape[:2]
    w_uk = w_kv_b[..., :NOPE_B]
    w_uv = w_kv_b[..., NOPE_B:]
    q_lat = jnp.einsum('bqhe,che->bhqc', q_nope, w_uk)

    def update(carry, k_a, v_a, c, kr, mask):
        carry_a, carry_b = carry
        s_a = jnp.einsum('bqhcd,bkhcd->bhcqk', qa, k_a).astype(f32) * SCALE_A
        s_b = (jnp.einsum('bhqc,bkc->bhqk', q_lat, c)
               + jnp.einsum('bqhr,bkr->bhqk', q_rope, kr)).astype(f32) * SCALE_B
        carry_a = _online_softmax_update(
            carry_a, s_a, mask, lambda p: jnp.einsum('bhcqk,bkhv->bhcqv', p, v_a.astype(f32)))
        carry_b = _online_softmax_update(
            carry_b, s_b, mask, lambda p: jnp.einsum('bhqk,bkc->bhqc', p, c.astype(f32)))
        return carry_a, carry_b

    init_a = (jnp.full((b, H_A, 2, t), NEG_INF, f32), jnp.zeros((b, H_A, 2, t), f32),
              jnp.zeros((b, H_A, 2, t, DV_A), f32))
    init_b = (jnp.full((b, H_B, t), NEG_INF, f32), jnp.zeros((b, H_B, t), f32),
              jnp.zeros((b, H_B, t, KV_LORA), f32))

    def page_step(carry, phys):
        return update(carry, cache_k[layer, phys], cache_v[layer, phys],
                      cache_c[layer, phys], cache_kr[layer, phys], None), None

    carry, _ = lax.scan(page_step, (init_a, init_b), page_table.T)
    causal = jnp.arange(t)[None, :] <= jnp.arange(t)[:, None]
    (_, l_a, acc_a), (_, l_b, acc_b) = update(carry, ka, va, c_kv, k_r, causal)
    o_a = jnp.transpose(acc_a / l_a[..., None], (0, 3, 1, 2, 4))
    o_b = jnp.einsum('bhqc,chv->bqhv', acc_b / l_b[..., None], w_uv.astype(f32))
    return o_a[..., 0, :], o_a[..., 1, :], o_b


def _diff_merge(o1, o2, lq1, lk1, lq2, lk2, g_sub, lam_init):
    f32 = jnp.float32
    lam = (jnp.exp(jnp.sum(lq1.astype(f32) * lk1.astype(f32)))
           - jnp.exp(jnp.sum(lq2.astype(f32) * lk2.astype(f32))) + lam_init)
    o = _rms(o1 - lam * o2, g_sub) * (1.0 - lam_init)
    return o.reshape(o.shape[0], o.shape[1], H_A * DV_A)


def _layer(x, pos, attend, lw, lam_init):
    (g_pre_mix, g_post_mix, g_pre_ffn, g_post_ffn, w_in, g_q_a, w_q_b, g_kv_a, w_kv_b,
     lq1, lk1, lq2, lk2, g_sub, w_out, w_up, w_down) = lw
    b, t, _ = x.shape
    h = _rms(x, g_pre_mix)
    qa, ka, va, q_nope, q_rope, c_kv, k_r = _project(h, pos, w_in, g_q_a, w_q_b, g_kv_a)
    o1, o2, o_b = attend(qa, ka, va, q_nope, q_rope, c_kv, k_r, w_kv_b)
    heads = jnp.concatenate(
        [_diff_merge(o1, o2, lq1, lk1, lq2, lk2, g_sub, lam_init).astype(x.dtype),
         o_b.reshape(b, t, H_B * DV_B).astype(x.dtype)], axis=-1)
    x = x + _rms(jnp.einsum('btm,md->btd', heads, w_out), g_post_mix)
    h = _rms(x, g_pre_ffn)
    u = jax.nn.relu(jnp.einsum('btd,df->btf', h, w_up))
    x = x + _rms(jnp.einsum('btf,fd->btd', u * u, w_down), g_post_ffn)
    return x, (ka, va, c_kv, k_r)


def setup_inputs(seed: int = 0) -> dict:
    key = jax.random.key(seed)
    ks = jax.random.split(key, 26)
    f32 = jnp.float32
    n_pages = PAST_LEN // PAGE_SIZE
    n_pool = (5 * DEC_BATCH * n_pages + 3) // 4

    def nrm(k, shape, scale):
        return jax.random.normal(k, shape, f32) * scale

    def gain(k, n):
        return 1.0 + 0.05 * jax.random.normal(k, (DEPTH, n), f32)

    page_table = jax.random.permutation(ks[6], n_pool)[:DEC_BATCH * n_pages]
    page_table = page_table.reshape(DEC_BATCH, n_pages).astype(jnp.int32)
    return {
        'x_prompt': jax.random.normal(ks[0], (BATCH, SEQ, D_MODEL), f32),
        'x_sample': jax.random.normal(ks[1], (DEC_BATCH, DEC_SEQ, D_MODEL), f32),
        'cache_diff_k': jax.random.normal(ks[2], (DEPTH, n_pool, PAGE_SIZE, H_A, 2, DH_A), f32),
        'cache_diff_v': jax.random.normal(ks[3], (DEPTH, n_pool, PAGE_SIZE, H_A, DV_A), f32),
        'cache_mla_latent': jax.random.normal(ks[4], (DEPTH, n_pool, PAGE_SIZE, KV_LORA), f32),
        'cache_mla_krope': jax.random.normal(ks[5], (DEPTH, n_pool, PAGE_SIZE, ROPE_B), f32),
        'page_table': page_table,
        'meta_tokens': jax.random.normal(ks[7], (N_META, D_MODEL), f32),
        'g_pre_mix': gain(ks[8], D_MODEL),
        'g_post_mix': gain(ks[9], D_MODEL),
        'g_pre_ffn': gain(ks[10], D_MODEL),
        'g_post_ffn': gain(ks[11], D_MODEL),
        'w_in': nrm(ks[12], (DEPTH, D_MODEL, IN_COLS), D_MODEL ** -0.5),
        'g_q_a': gain(ks[13], Q_LORA),
        'w_q_b': nrm(ks[14], (DEPTH, Q_LORA, H_B, NOPE_B + ROPE_B), Q_LORA ** -0.5),
        'g_kv_a': gain(ks[15], KV_LORA),
        'w_kv_b': nrm(ks[16], (DEPTH, KV_LORA, H_B, NOPE_B + DV_B), KV_LORA ** -0.5),
        'lambda_q1': nrm(ks[17], (DEPTH, DH_A), 0.1),
        'lambda_k1': nrm(ks[18], (DEPTH, DH_A), 0.1),
        'lambda_q2': nrm(ks[19], (DEPTH, DH_A), 0.1),
        'lambda_k2': nrm(ks[20], (DEPTH, DH_A), 0.1),
        'g_sub': gain(ks[21], DV_A),
        'w_out': nrm(ks[22], (DEPTH, MIX_WIDTH, D_MODEL), MIX_WIDTH ** -0.5),
        'w_up': nrm(ks[23], (DEPTH, D_MODEL, D_FF), D_MODEL ** -0.5),
        'w_down': nrm(ks[24], (DEPTH, D_FF, D_MODEL), D_FF ** -0.5),
    }


def reference(x_prompt, x_sample, cache_diff_k, cache_diff_v, cache_mla_latent, cache_mla_krope,
              page_table, meta_tokens, g_pre_mix, g_post_mix, g_pre_ffn, g_post_ffn, w_in, g_q_a,
              w_q_b, g_kv_a, w_kv_b, lambda_q1, lambda_k1, lambda_q2, lambda_k2, g_sub, w_out,
              w_up, w_down):
    b = x_prompt.shape[0]
    meta = jnp.broadcast_to(meta_tokens.astype(x_prompt.dtype)[None], (b, N_META, D_MODEL))
    hp = jnp.concatenate([meta, x_prompt], axis=1)
    pos_p = jnp.arange(hp.shape[1], dtype=jnp.int32)
    hs = x_sample
    pos_s = PAST_LEN + jnp.arange(x_sample.shape[1], dtype=jnp.int32)
    rows_p, rows_s = [], []
    for l in range(DEPTH):
        lw = (g_pre_mix[l], g_post_mix[l], g_pre_ffn[l], g_post_ffn[l], w_in[l], g_q_a[l],
              w_q_b[l], g_kv_a[l], w_kv_b[l], lambda_q1[l], lambda_k1[l], lambda_q2[l],
              lambda_k2[l], g_sub[l], w_out[l], w_up[l], w_down[l])
        lam_init = _lambda_init(l)
        hp, rp = _layer(hp, pos_p, _prompt_attention, lw, lam_init)
        sample_attend = functools.partial(
            _sample_attention, cache_k=cache_diff_k, cache_v=cache_diff_v, cache_c=cache_mla_latent,
            cache_kr=cache_mla_krope, page_table=page_table, layer=l)
        hs, rs = _layer(hs, pos_s, sample_attend, lw, lam_init)
        rows_p.append(rp)
        rows_s.append(rs)
    y_prompt = hp[:, N_META:]
    y_sample = hs
    p_diff_k = jnp.stack([r[0] for r in rows_p])
    p_diff_v = jnp.stack([r[1] for r in rows_p])
    p_mla_latent = jnp.stack([r[2] for r in rows_p])
    p_mla_krope = jnp.stack([r[3] for r in rows_p])
    s_diff_k = jnp.stack([r[0] for r in rows_s])
    s_diff_v = jnp.stack([r[1] for r in rows_s])
    s_mla_latent = jnp.stack([r[2] for r in rows_s])
    s_mla_krope = jnp.stack([r[3] for r in rows_s])
    return (y_prompt, y_sample, p_diff_k, p_diff_v, p_mla_latent, p_mla_krope,
            s_diff_k, s_diff_v, s_mla_latent, s_mla_krope)
```

```python
import functools
import math

import jax
import jax.numpy as jnp
import numpy as np
from jax import lax
from jax.experimental import pallas as pl
from jax.experimental.pallas import tpu as pltpu

D_MODEL = 1024
N_META = 16
H = 4
DH_A = 64
DV = 128
ROT_A = 16
Q_LORA = 384
KV_LORA = 256
NOPE_B = 64
ROPE_B = 32
D_FF = 4096
PAGE = 128
ROPE_THETA = 500000.0
RMS_EPS = 1e-6
NEG_INF = -1e30
SCALE_A = DH_A ** -0.5
SCALE_B = (NOPE_B + ROPE_B) ** -0.5
LAM_INIT = 0.8 - 0.6 * math.exp(-0.3 * 0)

LANE = 128
HW = H * LANE
IN_COLS = 2208
IN_COLS_PAD = 2304
C_QA, C_KA, C_VA, C_QL, C_CKV, C_KR = 0, 512, 1024, 1536, 1920, 2176

VMEM_LIMIT = 48 * 1024 * 1024

F32 = jnp.float32
BF16 = jnp.bfloat16


def _rms_f32(x, g):
    return x * lax.rsqrt(jnp.mean(x * x, axis=-1, keepdims=True) + RMS_EPS) * g


def _rot(x, c, sm, sp, half):
    w = x.shape[-1]
    return x * c + pltpu.roll(x, w - half, 1) * sm + pltpu.roll(x, half, 1) * sp


def _dot(a, b):
    return jnp.dot(a, b, preferred_element_type=F32)


def _dot_nt(a, b):
    return lax.dot_general(a, b, (((1,), (1,)), ((), ())), preferred_element_type=F32)


def _project_kernel(x_ref, g_ref, win_ref, gq_ref, wqb_ref, gkv_ref, wkvb_ref,
                    ca_ref, sma_ref, spa_ref, cb_ref, smb_ref, spb_ref,
                    qa_o, kaf_o, kab_o, vaf_o, vab_o, ckv_o, kr_o, qm_o, km_o, vb_o):
    h = _rms_f32(x_ref[...], g_ref[...]).astype(BF16)
    z = _dot(h, win_ref[...])
    ca, sma, spa = ca_ref[...], sma_ref[...], spa_ref[...]
    for j in range(H):
        sl = slice(j * LANE, (j + 1) * LANE)
        q = _rot(z[:, C_QA + j * LANE:C_QA + (j + 1) * LANE], ca, sma, spa, ROT_A // 2)
        k = _rot(z[:, C_KA + j * LANE:C_KA + (j + 1) * LANE], ca, sma, spa, ROT_A // 2)
        qa_o[:, sl] = (q * SCALE_A).astype(BF16)
        kaf_o[:, sl] = k
        kab_o[:, sl] = k.astype(BF16)
    va = z[:, C_VA:C_VA + HW]
    vaf_o[...] = va
    vab_o[...] = va.astype(BF16)

    cb, smb, spb = cb_ref[...], smb_ref[...], spb_ref[...]
    kr = _rot(z[:, C_KR:C_KR + LANE], cb, smb, spb, ROPE_B // 2)
    kr_o[...] = kr[:, :ROPE_B]
    kr_at64 = pltpu.roll(kr, NOPE_B, 1)

    ckv = _rms_f32(z[:, C_CKV:C_CKV + KV_LORA], gkv_ref[...])
    ckv_o[...] = ckv
    kv = _dot(ckv.astype(BF16), wkvb_ref[...])
    vb_o[...] = kv[:, HW:].astype(BF16)

    ql = _rms_f32(z[:, C_QL:C_QL + Q_LORA], gq_ref[...]).astype(BF16)
    qb = _dot(ql, wqb_ref[...])
    cq = pltpu.roll(cb, NOPE_B, 1)
    smq = pltpu.roll(smb, NOPE_B, 1)
    spq = pltpu.roll(spb, NOPE_B, 1)
    for j in range(H):
        sl = slice(j * LANE, (j + 1) * LANE)
        qm_o[:, sl] = _rot(qb[:, sl], cq, smq, spq, ROPE_B // 2).astype(BF16)
        km_o[:, sl] = (kv[:, sl] + kr_at64).astype(BF16)


def _rope_tables(pos):
    pos = pos.astype(F32)[:, None]
    lane = np.arange(LANE)

    def build(period, half, active):
        idx = lane % period
        inv = 1.0 / (ROPE_THETA ** (jnp.arange(half, dtype=F32) / half))
        ang = pos * inv[None, :]
        cos = jnp.cos(ang)[:, idx % half]
        sin = jnp.sin(ang)[:, idx % half]
        first = jnp.asarray(active & (idx < half))
        second = jnp.asarray(active & (idx >= half) & (idx < 2 * half))
        c = jnp.where(first | second, cos, 1.0)
        sm = jnp.where(first, -sin, 0.0)
        sp = jnp.where(second, sin, 0.0)
        return c, sm, sp

    a = build(DH_A, ROT_A // 2, np.ones(LANE, bool))
    b = build(LANE, ROPE_B // 2, lane < ROPE_B)
    return a + b


def _project(x, pos, tm, wts):
    n = x.shape[0]
    n_pos_tiles = pos.shape[0] // tm
    tables = _rope_tables(pos)
    g_pre, w_in, g_q, w_qb, g_kv, w_kvb = wts
    grid = (n // tm,)
    row = lambda w: pl.BlockSpec((tm, w), lambda i: (i, 0))
    full = lambda a: pl.BlockSpec(a.shape, lambda i: (0,) * a.ndim)
    tab = pl.BlockSpec((tm, LANE), lambda i: (i % n_pos_tiles, 0))
    outs = [(HW, BF16), (HW, F32), (HW, BF16), (HW, F32), (HW, BF16), (KV_LORA, F32),
            (ROPE_B, F32), (HW, BF16), (HW, BF16), (HW, BF16)]
    return pl.pallas_call(
        _project_kernel,
        grid=grid,
        in_specs=[row(D_MODEL), full(g_pre), full(w_in), full(g_q), full(w_qb), full(g_kv),
                  full(w_kvb)] + [tab] * 6,
        out_specs=[row(w) for w, _ in outs],
        out_shape=[jax.ShapeDtypeStruct((n, w), dt) for w, dt in outs],
        name="project",
        compiler_params=pltpu.CompilerParams(
            dimension_semantics=("arbitrary",), vmem_limit_bytes=VMEM_LIMIT),
    )(x, g_pre, w_in, g_q, w_qb, g_kv, w_kvb, *tables)


def _lambda(lq1, lk1, lq2, lk2):
    return (jnp.exp(jnp.sum(lq1[...] * lk1[...], keepdims=True))
            - jnp.exp(jnp.sum(lq2[...] * lk2[...], keepdims=True)) + LAM_INIT)


def _online(m_ref, l_ref, acc_ref, idx, s, v, mask):
    if mask is not None:
        s = jnp.where(mask, s, NEG_INF)
    m_old = m_ref[idx]
    m_new = jnp.maximum(m_old, jnp.max(s, axis=-1, keepdims=True))
    p = jnp.exp(s - m_new)
    if mask is not None:
        p = jnp.where(mask, p, 0.0)
    corr = jnp.exp(m_old - m_new)
    m_ref[idx] = m_new
    l_ref[idx] = l_ref[idx] * corr + jnp.sum(p, axis=-1, keepdims=True)
    acc_ref[idx] = acc_ref[idx] * corr + _dot(p.astype(BF16), v)


def _prompt_attend_kernel(qa_ref, qm_ref, ka_ref, va_ref, km_ref, vb_ref,
                          mka_ref, mva_ref, mkm_ref, mvb_ref,
                          gsub_ref, lq1, lk1, lq2, lk2,
                          out_ref, ma, la, acca, mb, lb, accb, *, tq):
    qi = pl.program_id(1)
    ki = pl.program_id(2)
    lane = lax.broadcasted_iota(jnp.int32, (tq, LANE), 1)

    def update(ka, va, km, vb, mask):
        mask2 = None if mask is None else jnp.concatenate([mask, mask], axis=0)
        for h in range(H):
            sl = slice(h * LANE, (h + 1) * LANE)
            q = qa_ref[0, :, sl]
            q12 = jnp.concatenate([jnp.where(lane < DH_A, q, 0), jnp.where(lane >= DH_A, q, 0)],
                                  axis=0)
            _online(ma, la, acca, h, _dot_nt(q12, ka[:, sl]), va[:, sl], mask2)
            s_b = _dot_nt(qm_ref[0, :, sl], km[:, sl]) * SCALE_B
            _online(mb, lb, accb, h, s_b, vb[:, sl], mask)

    @pl.when(ki == 0)
    def _():
        ma[...] = jnp.full(ma.shape, NEG_INF, F32)
        mb[...] = jnp.full(mb.shape, NEG_INF, F32)
        la[...] = jnp.zeros(la.shape, F32)
        lb[...] = jnp.zeros(lb.shape, F32)
        acca[...] = jnp.zeros(acca.shape, F32)
        accb[...] = jnp.zeros(accb.shape, F32)
        update(mka_ref[...], mva_ref[...], mkm_ref[...], mvb_ref[...], None)

    @pl.when(ki < qi)
    def _():
        update(ka_ref[0], va_ref[0], km_ref[0], vb_ref[0], None)

    @pl.when(ki == qi)
    def _():
        r = lax.broadcasted_iota(jnp.int32, (tq, tq), 0)
        c = lax.broadcasted_iota(jnp.int32, (tq, tq), 1)
        update(ka_ref[0], va_ref[0], km_ref[0], vb_ref[0], c <= r)
        lam = _lambda(lq1, lk1, lq2, lk2)
        for h in range(H):
            o12 = acca[h] / la[h]
            d = o12[:tq] - lam * o12[tq:]
            o = _rms_f32(d, gsub_ref[...]) * (1.0 - LAM_INIT)
            out_ref[0, :, h * LANE:(h + 1) * LANE] = o.astype(out_ref.dtype)
            out_ref[0, :, HW + h * LANE:HW + (h + 1) * LANE] = (accb[h] / lb[h]).astype(out_ref.dtype)


def _prompt_attend(px, pm, g_sub, lams, b, s, tq):
    qa, _, kab, _, vab, _, _, qm, km, vb = px
    _, _, mkab, _, mvab, _, _, _, mkm, mvb = pm
    nq = s // tq
    r3 = lambda a: a.reshape(b, s, HW)
    qspec = pl.BlockSpec((1, tq, HW), lambda bi, qi, ki: (bi, qi, 0))
    kspec = pl.BlockSpec((1, tq, HW), lambda bi, qi, ki: (bi, jnp.minimum(ki, qi), 0))
    full = lambda a: pl.BlockSpec(a.shape, lambda bi, qi, ki: (0,) * a.ndim)
    consts = [mkab, mvab, mkm, mvb, g_sub] + list(lams)
    return pl.pallas_call(
        functools.partial(_prompt_attend_kernel, tq=tq),
        grid=(b, nq, nq),
        in_specs=[qspec, qspec, kspec, kspec, kspec, kspec] + [full(a) for a in consts],
        out_specs=pl.BlockSpec((1, tq, 2 * HW), lambda bi, qi, ki: (bi, qi, 0)),
        out_shape=jax.ShapeDtypeStruct((b, s, 2 * HW), BF16),
        name="prompt_attend",
        scratch_shapes=[pltpu.VMEM((H, 2 * tq, 1), F32), pltpu.VMEM((H, 2 * tq, 1), F32),
                        pltpu.VMEM((H, 2 * tq, DV), F32),
                        pltpu.VMEM((H, tq, 1), F32), pltpu.VMEM((H, tq, 1), F32),
                        pltpu.VMEM((H, tq, DV), F32)],
        compiler_params=pltpu.CompilerParams(
            dimension_semantics=("arbitrary", "arbitrary", "arbitrary"),
            vmem_limit_bytes=VMEM_LIMIT),
    )(r3(qa), r3(qm), r3(kab), r3(vab), r3(km), r3(vb), *consts)


def _sample_attend_kernel(pt_ref, qa_ref, qm_ref, kan_ref, van_ref, cn_ref, krn_ref,
                          wabs_ref, wuv_ref, gsub_ref, lq1, lk1, lq2, lk2, *rest, pp, t):
    caches = rest[:4 * pp]
    out_ref, qbd, qext, ma, la, acca, mb, lb, accb = rest[4 * pp:]
    g = pl.program_id(1)
    ra, rb = 2 * t * H, t * H

    def rep_rows(q, rows, tok):
        out = jnp.broadcast_to(q[t - 1:t], (rows, HW))
        for i in range(t - 2, -1, -1):
            out = jnp.where(tok == i, jnp.broadcast_to(q[i:i + 1], (rows, HW)), out)
        return out

    @pl.when(g == 0)
    def _():
        r = lax.broadcasted_iota(jnp.int32, (ra, HW), 0)
        ln = lax.broadcasted_iota(jnp.int32, (ra, HW), 1)
        q = rep_rows(qa_ref[0].astype(F32), ra, (r // H) % t)
        chunk = (r % H) * 2 + r // (t * H)
        qbd[...] = jnp.where(ln // DH_A == chunk, q, 0.0).astype(BF16)
        r = lax.broadcasted_iota(jnp.int32, (rb, HW), 0)
        ln = lax.broadcasted_iota(jnp.int32, (rb, HW), 1)
        q = rep_rows(qm_ref[0].astype(F32), rb, r // H)
        q = jnp.where(ln // LANE == r % H, q, 0.0).astype(BF16)
        qext[...] = _dot(q, wabs_ref[...]).astype(BF16)
        ma[...] = jnp.full(ma.shape, NEG_INF, F32)
        mb[...] = jnp.full(mb.shape, NEG_INF, F32)
        la[...] = jnp.zeros(la.shape, F32)
        lb[...] = jnp.zeros(lb.shape, F32)
        acca[...] = jnp.zeros(acca.shape, F32)
        accb[...] = jnp.zeros(accb.shape, F32)

    def pad_rows(x, rows):
        r = lax.broadcasted_iota(jnp.int32, (rows, x.shape[1]), 0)
        out = jnp.zeros((rows, x.shape[1]), x.dtype)
        for i in range(t):
            out = jnp.where(r == i, jnp.broadcast_to(x[i:i + 1], out.shape), out)
        return out

    def update(k, v, c, kr, mask_a, mask_b):
        k, v, c, kr = k.astype(BF16), v.astype(BF16), c.astype(BF16), kr.astype(BF16)
        _online(ma, la, acca, 0, _dot_nt(qbd[...], k), v, mask_a)
        qe = qext[...]
        s_b = (_dot_nt(qe[:, :KV_LORA], c) + _dot_nt(qe[:, KV_LORA:], kr)) * SCALE_B
        _online(mb, lb, accb, 0, s_b, c, mask_b)

    for p in range(pp):
        update(caches[p][0], caches[pp + p][0], caches[2 * pp + p][0], caches[3 * pp + p][0],
               None, None)

    @pl.when(g == pl.num_programs(1) - 1)
    def _():
        nk = 16
        ra_i = lax.broadcasted_iota(jnp.int32, (ra, nk), 0)
        ca_i = lax.broadcasted_iota(jnp.int32, (ra, nk), 1)
        rb_i = lax.broadcasted_iota(jnp.int32, (rb, nk), 0)
        cb_i = lax.broadcasted_iota(jnp.int32, (rb, nk), 1)
        update(pad_rows(kan_ref[0], nk), pad_rows(van_ref[0], nk), pad_rows(cn_ref[0], nk),
               pad_rows(krn_ref[0], nk), ca_i <= (ra_i // H) % t, cb_i <= rb_i // H)
        lam = _lambda(lq1, lk1, lq2, lk2)
        r = lax.broadcasted_iota(jnp.int32, (rb, HW), 0)
        ln = lax.broadcasted_iota(jnp.int32, (rb, HW), 1)
        own = ln // LANE == r % H
        oa = acca[0] / la[0]
        d = jnp.where(own, oa[:rb] - lam * oa[rb:], 0.0)
        gs = gsub_ref[...]
        gsub = jnp.concatenate([gs] * H, axis=1)
        ms = jnp.sum(d * d, axis=-1, keepdims=True) * (1.0 / DV)
        o_a = d * lax.rsqrt(ms + RMS_EPS) * gsub * (1.0 - LAM_INIT)
        o_b = _dot((accb[0] / lb[0]).astype(BF16), wuv_ref[...])
        o_b = jnp.where(own, o_b, 0.0)
        tr = lax.broadcasted_iota(jnp.int32, (rb, rb), 0)
        tc = lax.broadcasted_iota(jnp.int32, (rb, rb), 1)
        gather = (tc // H == tr).astype(BF16)
        out_ref[0, :, :HW] = _dot(gather, o_a.astype(BF16))[:t]
        out_ref[0, :, HW:] = _dot(gather, o_b.astype(BF16))[:t]


def _sample_attend(ps, caches, page_table, wabs, wuv, g_sub, lams, b, t, pp):
    qa, kaf, _, vaf, _, ckv, kr, qm, _, _ = ps
    n_pages = page_table.shape[1]
    r3 = lambda a: a.reshape(b, t, a.shape[-1])
    tok = lambda w: pl.BlockSpec((1, t, w), lambda bi, g, pt: (bi, 0, 0))
    full = lambda a: pl.BlockSpec(a.shape, lambda bi, g, pt: (0,) * a.ndim)

    def page(w, p):
        return pl.BlockSpec((1, PAGE, w), lambda bi, g, pt: (pt[bi, g * pp + p], 0, 0))

    consts = [wabs, wuv, g_sub] + list(lams)
    cache_specs, cache_args = [], []
    for a in caches:
        for p in range(pp):
            cache_specs.append(page(a.shape[-1], p))
            cache_args.append(a)
    ra, rb = 2 * t * H, t * H
    grid_spec = pltpu.PrefetchScalarGridSpec(
        num_scalar_prefetch=1,
        grid=(b, n_pages // pp),
        in_specs=[tok(HW), tok(HW), tok(HW), tok(HW), tok(KV_LORA), tok(ROPE_B)]
                 + [full(a) for a in consts] + cache_specs,
        out_specs=pl.BlockSpec((1, t, 2 * HW), lambda bi, g, pt: (bi, 0, 0)),
        scratch_shapes=[pltpu.VMEM((ra, HW), BF16), pltpu.VMEM((rb, KV_LORA + ROPE_B), BF16),
                        pltpu.VMEM((1, ra, 1), F32), pltpu.VMEM((1, ra, 1), F32),
                        pltpu.VMEM((1, ra, HW), F32),
                        pltpu.VMEM((1, rb, 1), F32), pltpu.VMEM((1, rb, 1), F32),
                        pltpu.VMEM((1, rb, KV_LORA), F32)])
    return pl.pallas_call(
        functools.partial(_sample_attend_kernel, pp=pp, t=t),
        grid_spec=grid_spec,
        out_shape=jax.ShapeDtypeStruct((b, t, 2 * HW), F32),
        name="sample_attend",
        compiler_params=pltpu.CompilerParams(
            dimension_semantics=("arbitrary", "arbitrary"), vmem_limit_bytes=VMEM_LIMIT),
    )(page_table, r3(qa.astype(F32)), r3(qm.astype(F32)), r3(kaf), r3(vaf), r3(ckv), r3(kr),
      *consts, *cache_args)


def _mix_ffn_kernel(x_ref, hd_ref, wout_ref, wup_ref, wdn_ref, gpm_ref, gpf_ref, gqf_ref, y_ref):
    mix = _dot(hd_ref[...].astype(BF16), wout_ref[...])
    x1 = x_ref[...] + _rms_f32(mix, gpm_ref[...])
    h = _rms_f32(x1, gpf_ref[...]).astype(BF16)
    u = jnp.maximum(_dot(h, wup_ref[...]), 0.0)
    f = _dot((u * u).astype(BF16), wdn_ref[...])
    y_ref[...] = x1 + _rms_f32(f, gqf_ref[...])


def _mix_ffn(x, heads, w_out, w_up, w_down, g_post_mix, g_pre_ffn, g_post_ffn, tm):
    n = x.shape[0]
    tm = min(tm, n)
    assert n % tm == 0
    row = lambda w: pl.BlockSpec((tm, w), lambda i: (i, 0))
    const =lambda a: pl.BlockSpec(a.shape, lambda i: (0,) * a.ndim, pipeline_mode=pl.Buffered(1))
    consts = [w_out, w_up, w_down, g_post_mix, g_pre_ffn, g_post_ffn]
    return pl.pallas_call(
        _mix_ffn_kernel,
        grid=(n // tm,),
        in_specs=[row(D_MODEL), row(2 * HW)] + [const(a) for a in consts],
        out_specs=row(D_MODEL),
        out_shape=jax.ShapeDtypeStruct((n, D_MODEL), F32),
        name="mix_ffn",
        compiler_params=pltpu.CompilerParams(
            dimension_semantics=("arbitrary",), vmem_limit_bytes=VMEM_LIMIT),
    )(x, heads, *consts)


def _prep_weights(w_in, w_q_b, w_kv_b):
    w_in_p = jnp.pad(w_in, ((0, 0), (0, IN_COLS_PAD - IN_COLS))).astype(BF16)
    w_qb_p = jnp.pad(w_q_b, ((0, 0), (0, 0), (0, LANE - NOPE_B - ROPE_B))).reshape(Q_LORA, HW)
    w_uk = w_kv_b[..., :NOPE_B]
    w_uv = w_kv_b[..., NOPE_B:]
    w_uk_p = jnp.pad(w_uk, ((0, 0), (0, 0), (0, LANE - NOPE_B))).reshape(KV_LORA, HW)
    w_uv_f = w_uv.reshape(KV_LORA, HW)
    w_kvb_p = jnp.concatenate([w_uk_p, w_uv_f], axis=1)
    eye_r = jnp.eye(ROPE_B, dtype=w_in.dtype)
    per_head = []
    for h in range(H):
        lat = jnp.concatenate([w_uk[:, h, :].T, jnp.zeros((LANE - NOPE_B, KV_LORA), w_in.dtype)], 0)
        rope = jnp.concatenate([jnp.zeros((NOPE_B, ROPE_B), w_in.dtype), eye_r,
                                jnp.zeros((LANE - NOPE_B - ROPE_B, ROPE_B), w_in.dtype)], 0)
        per_head.append(jnp.concatenate([lat, rope], axis=1))
    w_abs = jnp.concatenate(per_head, axis=0)
    return (w_in_p, w_qb_p.astype(BF16), w_kvb_p.astype(BF16), w_abs.astype(BF16),
            w_uv_f.astype(BF16))


def kernel(x_prompt, x_sample, cache_diff_k, cache_diff_v, cache_mla_latent, cache_mla_krope,
           page_table, meta_tokens, g_pre_mix, g_post_mix, g_pre_ffn, g_post_ffn, w_in, g_q_a,
           w_q_b, g_kv_a, w_kv_b, lambda_q1, lambda_k1, lambda_q2, lambda_k2, g_sub, w_out,
           w_up, w_down):
    b, s, d = x_prompt.shape
    db, t, _ = x_sample.shape
    n_pool = cache_diff_k.shape[1]
    past = page_table.shape[1] * PAGE
    layer = 0

    w_in_p, w_qb_p, w_kvb_p, w_abs, w_uv_f = _prep_weights(w_in[layer], w_q_b[layer], w_kv_b[layer])
    proj_w = (g_pre_mix, w_in_p, g_q_a, w_qb_p, g_kv_a, w_kvb_p)
    lams = (lambda_q1, lambda_k1, lambda_q2, lambda_k2)

    tm = 512
    px = _project(x_prompt.reshape(b * s, d), N_META + jnp.arange(s), tm, proj_w)
    pm = _project(meta_tokens.astype(x_prompt.dtype), jnp.arange(N_META), N_META, proj_w)
    tms = min(tm, db * t)
    ps = _project(x_sample.reshape(db * t, d), past + (jnp.arange(tms) % t), tms, proj_w)

    heads_p = _prompt_attend(px, pm, g_sub, lams, b, s, tq=512)
    caches = (cache_diff_k[layer].reshape(n_pool, PAGE, HW),
              cache_diff_v[layer].reshape(n_pool, PAGE, HW),
              cache_mla_latent[layer], cache_mla_krope[layer])
    heads_s = _sample_attend(ps, caches, page_table, w_abs, w_uv_f, g_sub, lams, db, t, pp=8)

    ffn_w = (w_out[layer].astype(BF16), w_up[layer].astype(BF16), w_down[layer].astype(BF16),
             g_post_mix, g_pre_ffn, g_post_ffn)
    y_prompt = _mix_ffn(x_prompt.reshape(b * s, d), heads_p.reshape(b * s, 2 * HW), *ffn_w, tm=512)
    y_sample = _mix_ffn(x_sample.reshape(db * t, d), heads_s.reshape(db * t, 2 * HW), *ffn_w, tm=512)

    def with_meta(x_part, m_part, tail):
        m = jnp.broadcast_to(m_part[None], (b, N_META, m_part.shape[-1]))
        full = jnp.concatenate([m, x_part.reshape(b, s, -1)], axis=1)
        return full.reshape((1, b, s + N_META) + tail)

    k_shape, v_shape = (H, 2, DH_A), (H, DV)
    return (y_prompt.reshape(b, s, d), y_sample.reshape(db, t, d),
            with_meta(px[1], pm[1], k_shape), with_meta(px[3], pm[3], v_shape),
            with_meta(px[5], pm[5], (KV_LORA,)), with_meta(px[6], pm[6], (ROPE_B,)),
            ps[1].reshape((1, db, t) + k_shape), ps[3].reshape((1, db, t) + v_shape),
            ps[5].reshape(1, db, t, KV_LORA), ps[6].reshape(1, db, t, ROPE_B))
```

```python
import functools
import math

import jax
import jax.numpy as jnp
import numpy as np
from jax import lax
from jax.experimental import pallas as pl
from jax.experimental.pallas import tpu as pltpu

D_MODEL = 1024
N_META = 16
H = 4
DH_A = 64
DV = 128
ROT_A = 16
Q_LORA = 384
KV_LORA = 256
NOPE_B = 64
ROPE_B = 32
D_FF = 4096
PAGE = 128
ROPE_THETA = 500000.0
RMS_EPS = 1e-6
NEG_INF = -1e30
SCALE_A = DH_A ** -0.5
SCALE_B = (NOPE_B + ROPE_B) ** -0.5
LAM_INIT = 0.8 - 0.6 * math.exp(-0.3 * 0)

LANE = 128
HW = H * LANE
IN_COLS = 2208
IN_COLS_PAD = 2304
C_QA, C_KA, C_VA, C_QL, C_CKV, C_KR = 0, 512, 1024, 1536, 1920, 2176

VMEM_LIMIT = 48 * 1024 * 1024

F32 = jnp.float32
BF16 = jnp.bfloat16


def _rms_f32(x, g):
    return x * lax.rsqrt(jnp.mean(x * x, axis=-1, keepdims=True) + RMS_EPS) * g


def _rot(x, c, sm, sp, half):
    w = x.shape[-1]
    return x * c + pltpu.roll(x, w - half, 1) * sm + pltpu.roll(x, half, 1) * sp


def _dot(a, b):
    return jnp.dot(a, b, preferred_element_type=F32)


def _dot_nt(a, b):
    return lax.dot_general(a, b, (((1,), (1,)), ((), ())), preferred_element_type=F32)


def _project_kernel(x_ref, g_ref, win_ref, gq_ref, wqb_ref, gkv_ref, wuk_ref, wvat_ref, wuvt_ref,
                    ca_ref, sma_ref, spa_ref, cb_ref, smb_ref, spb_ref,
                    qa_o, kaf_o, kab_o, vaf_o, ckv_o, kr_o, qm_o, km_o, vat_o, vbt_o):
    h = _rms_f32(x_ref[...], g_ref[...]).astype(BF16)
    z = _dot(h, win_ref[...])
    ca, sma, spa = ca_ref[...], sma_ref[...], spa_ref[...]
    for j in range(H):
        sl = slice(j * LANE, (j + 1) * LANE)
        q = _rot(z[:, C_QA + j * LANE:C_QA + (j + 1) * LANE], ca, sma, spa, ROT_A // 2)
        k = _rot(z[:, C_KA + j * LANE:C_KA + (j + 1) * LANE], ca, sma, spa, ROT_A // 2)
        qa_o[:, sl] = (q * SCALE_A).astype(BF16)
        kaf_o[:, sl] = k
        kab_o[:, sl] = k.astype(BF16)
    vaf_o[...] = z[:, C_VA:C_VA + HW]
    vat_o[0] = _dot_nt(wvat_ref[...], h).astype(BF16)

    cb, smb, spb = cb_ref[...], smb_ref[...], spb_ref[...]
    kr = _rot(z[:, C_KR:C_KR + LANE], cb, smb, spb, ROPE_B // 2)
    kr_o[...] = kr[:, :ROPE_B]
    kr_at64 = pltpu.roll(kr, NOPE_B, 1)

    ckv = _rms_f32(z[:, C_CKV:C_CKV + KV_LORA], gkv_ref[...])
    ckv_o[...] = ckv
    ckv_b = ckv.astype(BF16)
    kn = _dot(ckv_b, wuk_ref[...])
    vbt_o[0] = _dot_nt(wuvt_ref[...], ckv_b).astype(BF16)

    ql = _rms_f32(z[:, C_QL:C_QL + Q_LORA], gq_ref[...]).astype(BF16)
    qb = _dot(ql, wqb_ref[...])
    cq = pltpu.roll(cb, NOPE_B, 1)
    smq = pltpu.roll(smb, NOPE_B, 1)
    spq = pltpu.roll(spb, NOPE_B, 1)
    for j in range(H):
        sl = slice(j * LANE, (j + 1) * LANE)
        qm_o[:, sl] = _rot(qb[:, sl], cq, smq, spq, ROPE_B // 2).astype(BF16)
        km_o[:, sl] = (kn[:, sl] + kr_at64).astype(BF16)


def _rope_tables(pos):
    pos = pos.astype(F32)[:, None]
    lane = np.arange(LANE)

    def build(period, half, active):
        idx = lane % period
        inv = 1.0 / (ROPE_THETA ** (jnp.arange(half, dtype=F32) / half))
        ang = pos * inv[None, :]
        cos = jnp.cos(ang)[:, idx % half]
        sin = jnp.sin(ang)[:, idx % half]
        first = jnp.asarray(active & (idx < half))
        second = jnp.asarray(active & (idx >= half) & (idx < 2 * half))
        c = jnp.where(first | second, cos, 1.0)
        sm = jnp.where(first, -sin, 0.0)
        sp = jnp.where(second, sin, 0.0)
        return c, sm, sp

    a = build(DH_A, ROT_A // 2, np.ones(LANE, bool))
    b = build(LANE, ROPE_B // 2, lane < ROPE_B)
    return a + b


def _project(x, pos, tm, nb, wts):
    n = x.shape[0]
    n_pos_tiles = pos.shape[0] // tm
    tiles_per_b = n // nb // tm
    tables = _rope_tables(pos)
    grid = (n // tm,)
    row = lambda w: pl.BlockSpec((tm, w), lambda i: (i, 0))
    full = lambda a: pl.BlockSpec(a.shape, lambda i: (0,) * a.ndim)
    tab = pl.BlockSpec((tm, LANE), lambda i: (i % n_pos_tiles, 0))
    colmajor = pl.BlockSpec((1, HW, tm), lambda i: (i // tiles_per_b, 0, i % tiles_per_b))
    outs = [(HW, BF16), (HW, F32), (HW, BF16), (HW, F32), (KV_LORA, F32), (ROPE_B, F32),
            (HW, BF16), (HW, BF16)]
    t_shape = jax.ShapeDtypeStruct((nb, HW, n // nb), BF16)
    return pl.pallas_call(
        _project_kernel,
        grid=grid,
        in_specs=[row(D_MODEL)] + [full(a) for a in wts] + [tab] * 6,
        out_specs=[row(w) for w, _ in outs] + [colmajor, colmajor],
        out_shape=[jax.ShapeDtypeStruct((n, w), dt) for w, dt in outs] + [t_shape, t_shape],
        name="project",
        compiler_params=pltpu.CompilerParams(
            dimension_semantics=("arbitrary",), vmem_limit_bytes=VMEM_LIMIT),
    )(x, *wts, *tables)


def _lambda(lq1, lk1, lq2, lk2):
    return (jnp.exp(jnp.sum(lq1[...] * lk1[...], keepdims=True))
            - jnp.exp(jnp.sum(lq2[...] * lk2[...], keepdims=True)) + LAM_INIT)


def _softmax_step(m_ref, l_ref, idx, s, mask, axis):
    if mask is not None:
        s = jnp.where(mask, s, NEG_INF)
    m_old = m_ref[idx]
    m_new = jnp.maximum(m_old, jnp.max(s, axis=axis, keepdims=True))
    p = jnp.exp(s - m_new)
    if mask is not None:
        p = jnp.where(mask, p, 0.0)
    corr = jnp.exp(m_old - m_new)
    m_ref[idx] = m_new
    l_ref[idx] = l_ref[idx] * corr + jnp.sum(p, axis=axis, keepdims=True)
    return p, corr


def _prompt_attend_kernel(qa_ref, qm_ref, ka_ref, km_ref, vat_ref, vbt_ref,
                          mka_ref, mkm_ref, mvat_ref, mvbt_ref,
                          gcol_ref, lq1, lk1, lq2, lk2,
                          out_ref, qw, ma, la, acca, mb, lb, accb, *, tq, sk):
    qi = pl.program_id(1)
    ki = pl.program_id(2)

    def update(ka, km, vat, vbt, mask):
        mask2 = None if mask is None else jnp.concatenate([mask, mask], axis=1)
        for h in range(H):
            sl = slice(h * LANE, (h + 1) * LANE)
            p, corr = _softmax_step(ma, la, h, _dot_nt(ka[:, sl], qw[h]), mask2, 0)
            acca[h] = acca[h] * corr + _dot(vat[sl, :], p.astype(BF16))
            s_b = _dot_nt(km[:, sl], qm_ref[0, :, sl]) * SCALE_B
            p, corr = _softmax_step(mb, lb, h, s_b, mask, 0)
            accb[h] = accb[h] * corr + _dot(vbt[sl, :], p.astype(BF16))

    @pl.when(ki == 0)
    def _():
        lane = lax.broadcasted_iota(jnp.int32, (tq, LANE), 1)
        for h in range(H):
            q = qa_ref[0, :, h * LANE:(h + 1) * LANE]
            qw[h, :tq] = jnp.where(lane < DH_A, q, 0)
            qw[h, tq:] = jnp.where(lane >= DH_A, q, 0)
        ma[...] = jnp.full(ma.shape, NEG_INF, F32)
        mb[...] = jnp.full(mb.shape, NEG_INF, F32)
        la[...] = jnp.zeros(la.shape, F32)
        lb[...] = jnp.zeros(lb.shape, F32)
        acca[...] = jnp.zeros(acca.shape, F32)
        accb[...] = jnp.zeros(accb.shape, F32)
        update(mka_ref[...], mkm_ref[...], mvat_ref[0], mvbt_ref[0], None)

    @pl.when(ki < qi)
    def _():
        for j in range(tq // sk):
            rows = slice(j * sk, (j + 1) * sk)
            update(ka_ref[0, rows, :], km_ref[0, rows, :], vat_ref[0, :, rows], vbt_ref[0, :, rows],
                   None)

    @pl.when(ki == qi)
    def _():
        r = lax.broadcasted_iota(jnp.int32, (sk, tq), 0)
        c = lax.broadcasted_iota(jnp.int32, (sk, tq), 1)
        for j in range(tq // sk):
            rows = slice(j * sk, (j + 1) * sk)
            update(ka_ref[0, rows, :], km_ref[0, rows, :], vat_ref[0, :, rows], vbt_ref[0, :, rows],
                   r + j * sk <= c)
        lam = _lambda(lq1, lk1, lq2, lk2)
        for h in range(H):
            o12 = acca[h] / la[h]
            d = o12[:, :tq] - lam * o12[:, tq:]
            ms = jnp.mean(d * d, axis=0, keepdims=True)
            o = d * lax.rsqrt(ms + RMS_EPS) * gcol_ref[...] * (1.0 - LAM_INIT)
            out_ref[0, :, h * LANE:(h + 1) * LANE] = o.T.astype(out_ref.dtype)
            ob = accb[h] / lb[h]
            out_ref[0, :, HW + h * LANE:HW + (h + 1) * LANE] = ob.T.astype(out_ref.dtype)


def _prompt_attend(px, pm, g_sub, lams, b, s, tq, sk):
    qa, _, kab, _, _, _, qm, km, vat, vbt = px
    _, _, mkab, _, _, _, _, mkm, mvat, mvbt = pm
    nq = s // tq
    r3 = lambda a: a.reshape(b, s, HW)
    qspec = pl.BlockSpec((1, tq, HW), lambda bi, qi, ki: (bi, qi, 0))
    kspec = pl.BlockSpec((1, tq, HW), lambda bi, qi, ki: (bi, jnp.minimum(ki, qi), 0))
    vspec = pl.BlockSpec((1, HW, tq), lambda bi, qi, ki: (bi, 0, jnp.minimum(ki, qi)))
    full = lambda a: pl.BlockSpec(a.shape, lambda bi, qi, ki: (0,) * a.ndim)
    gcol = jnp.broadcast_to(g_sub.reshape(DV, 1), (DV, tq))
    consts = [mkab, mkm, mvat, mvbt, gcol] + list(lams)
    return pl.pallas_call(
        functools.partial(_prompt_attend_kernel, tq=tq, sk=sk),
        grid=(b, nq, nq),
        in_specs=[qspec, qspec, kspec, kspec, vspec, vspec] + [full(a) for a in consts],
        out_specs=pl.BlockSpec((1, tq, 2 * HW), lambda bi, qi, ki: (bi, qi, 0)),
        out_shape=jax.ShapeDtypeStruct((b, s, 2 * HW), BF16),
        name="prompt_attend",
        scratch_shapes=[pltpu.VMEM((H, 2 * tq, LANE), BF16),
                        pltpu.VMEM((H, 1, 2 * tq), F32), pltpu.VMEM((H, 1, 2 * tq), F32),
                        pltpu.VMEM((H, DV, 2 * tq), F32),
                        pltpu.VMEM((H, 1, tq), F32), pltpu.VMEM((H, 1, tq), F32),
                        pltpu.VMEM((H, DV, tq), F32)],
        compiler_params=pltpu.CompilerParams(
            dimension_semantics=("arbitrary", "arbitrary", "arbitrary"),
            vmem_limit_bytes=VMEM_LIMIT),
    )(r3(qa), r3(qm), r3(kab), r3(km), vat, vbt, *consts)


def _sample_attend_kernel(pt_ref, qa_ref, qm_ref, kan_ref, van_ref, cn_ref, krn_ref,
                          wabs_ref, wuv_ref, gsub_ref, lq1, lk1, lq2, lk2, *rest, pp, t):
    kt_refs, v_refs, c_refs, krt_refs = (rest[i * pp:(i + 1) * pp] for i in range(4))
    out_ref, qbd, qext, ma, la, acca, mb, lb, accb = rest[4 * pp:]
    g = pl.program_id(1)
    ra, rb = 2 * H * t, H * t
    head_a = (lax.broadcasted_iota(jnp.int32, (ra, LANE), 0) // t) % H

    def rep_rows(q, rows):
        tok = lax.broadcasted_iota(jnp.int32, (rows, HW), 0) % t
        out = jnp.broadcast_to(q[t - 1:t], (rows, HW))
        for i in range(t - 2, -1, -1):
            out = jnp.where(tok == i, jnp.broadcast_to(q[i:i + 1], (rows, HW)), out)
        return out

    @pl.when(g == 0)
    def _():
        r = lax.broadcasted_iota(jnp.int32, (ra, HW), 0)
        ln = lax.broadcasted_iota(jnp.int32, (ra, HW), 1)
        chunk = ((r // t) % H) * 2 + r // (H * t)
        qbd[...] = jnp.where(ln // DH_A == chunk, rep_rows(qa_ref[0], ra), 0.0).astype(BF16)
        r = lax.broadcasted_iota(jnp.int32, (rb, HW), 0)
        ln = lax.broadcasted_iota(jnp.int32, (rb, HW), 1)
        q = jnp.where(ln // LANE == r // t, rep_rows(qm_ref[0], rb), 0.0).astype(BF16)
        qext[...] = _dot(q, wabs_ref[...]).astype(BF16)
        ma[...] = jnp.full(ma.shape, NEG_INF, F32)
        mb[...] = jnp.full(mb.shape, NEG_INF, F32)
        la[...] = jnp.zeros(la.shape, F32)
        lb[...] = jnp.zeros(lb.shape, F32)
        acca[...] = jnp.zeros(acca.shape, F32)
        accb[...] = jnp.zeros(accb.shape, F32)

    def accumulate(p_a, corr_a, v_of_head, p_b, corr_b, c):
        p_a = p_a.astype(BF16)
        new = jnp.zeros((ra, DV), F32)
        for h in range(H):
            new = jnp.where(head_a == h, _dot(p_a, v_of_head(h)), new)
        acca[0] = acca[0] * corr_a + new
        accb[0] = accb[0] * corr_b + _dot(p_b.astype(BF16), c)

    qe = qext[...]
    kt = jnp.concatenate([r[0].astype(BF16) for r in kt_refs], axis=1)
    krt = jnp.concatenate([r[0].astype(BF16) for r in krt_refs], axis=1)
    c = jnp.concatenate([r[0].astype(BF16) for r in c_refs], axis=0)
    s_b = (_dot_nt(qe[:, :KV_LORA], c) + _dot(qe[:, KV_LORA:], krt)) * SCALE_B
    p_a, corr_a = _softmax_step(ma, la, 0, _dot(qbd[...], kt), None, 1)
    p_b, corr_b = _softmax_step(mb, lb, 0, s_b, None, 1)

    def page_values(h):
        return jnp.concatenate([r[0, pl.ds(h, PAGE, stride=H), :].astype(BF16) for r in v_refs],
                               axis=0)

    accumulate(p_a, corr_a, page_values, p_b, corr_b, c)

    @pl.when(g == pl.num_programs(1) - 1)
    def _():
        nk = 16

        def pad_rows(x):
            r = lax.broadcasted_iota(jnp.int32, (nk, x.shape[1]), 0)
            out = jnp.zeros((nk, x.shape[1]), x.dtype)
            for i in range(t):
                out = jnp.where(r == i, jnp.broadcast_to(x[i:i + 1], out.shape), out)
            return out.astype(BF16)

        kn, vn, cn, krn = (pad_rows(r[0]) for r in (kan_ref, van_ref, cn_ref, krn_ref))
        causal_a = (lax.broadcasted_iota(jnp.int32, (ra, nk), 1)
                    <= lax.broadcasted_iota(jnp.int32, (ra, nk), 0) % t)
        causal_b = (lax.broadcasted_iota(jnp.int32, (rb, nk), 1)
                    <= lax.broadcasted_iota(jnp.int32, (rb, nk), 0) % t)
        s_n = (_dot_nt(qe[:, :KV_LORA], cn) + _dot_nt(qe[:, KV_LORA:], krn)) * SCALE_B
        pn_a, cr_a = _softmax_step(ma, la, 0, _dot_nt(qbd[...], kn), causal_a, 1)
        pn_b, cr_b = _softmax_step(mb, lb, 0, s_n, causal_b, 1)
        accumulate(pn_a, cr_a, lambda h: vn[:, h * LANE:(h + 1) * LANE], pn_b, cr_b, cn)

        lam = _lambda(lq1, lk1, lq2, lk2)
        oa = acca[0] / la[0]
        o_a = _rms_f32(oa[:rb] - lam * oa[rb:], gsub_ref[...]) * (1.0 - LAM_INIT)
        o_b = _dot((accb[0] / lb[0]).astype(BF16), wuv_ref[...])
        for h in range(H):
            rows = slice(h * t, (h + 1) * t)
            out_ref[0, :, h * LANE:(h + 1) * LANE] = o_a[rows, :]
            out_ref[0, :, HW + h * LANE:HW + (h + 1) * LANE] = o_b[rows, h * LANE:(h + 1) * LANE]


def _sample_attend(ps, caches, page_table, wabs, wuv, g_sub, lams, b, t, pp):
    qa, kaf, _, vaf, ckv, kr, qm, _, _, _ = ps
    n_pages = page_table.shape[1]
    r3 = lambda a: a.reshape(b, t, a.shape[-1])
    tok = lambda w: pl.BlockSpec((1, t, w), lambda bi, g, pt: (bi, 0, 0))
    full = lambda a: pl.BlockSpec(a.shape, lambda bi, g, pt: (0,) * a.ndim)

    def page(shape, p):
        return pl.BlockSpec((1,) + shape, lambda bi, g, pt: (pt[bi, g * pp + p], 0, 0))

    consts = [wabs, wuv, g_sub] + list(lams)
    cache_specs, cache_args = [], []
    for a in caches:
        for p in range(pp):
            cache_specs.append(page(a.shape[1:], p))
            cache_args.append(a)
    ra, rb = 2 * t * H, t * H
    grid_spec = pltpu.PrefetchScalarGridSpec(
        num_scalar_prefetch=1,
        grid=(b, n_pages // pp),
        in_specs=[tok(HW), tok(HW), tok(HW), tok(HW), tok(KV_LORA), tok(ROPE_B)]
                 + [full(a) for a in consts] + cache_specs,
        out_specs=pl.BlockSpec((1, t, 2 * HW), lambda bi, g, pt: (bi, 0, 0)),
        scratch_shapes=[pltpu.VMEM((ra, HW), BF16), pltpu.VMEM((rb, KV_LORA + ROPE_B), BF16),
                        pltpu.VMEM((1, ra, 1), F32), pltpu.VMEM((1, ra, 1), F32),
                        pltpu.VMEM((1, ra, DV), F32),
                        pltpu.VMEM((1, rb, 1), F32), pltpu.VMEM((1, rb, 1), F32),
                        pltpu.VMEM((1, rb, KV_LORA), F32)])
    return pl.pallas_call(
        functools.partial(_sample_attend_kernel, pp=pp, t=t),
        grid_spec=grid_spec,
        out_shape=jax.ShapeDtypeStruct((b, t, 2 * HW), F32),
        name="sample_attend",
        compiler_params=pltpu.CompilerParams(
            dimension_semantics=("arbitrary", "arbitrary"), vmem_limit_bytes=VMEM_LIMIT),
    )(page_table, r3(qa.astype(F32)), r3(qm.astype(F32)), r3(kaf), r3(vaf), r3(ckv), r3(kr),
      *consts, *cache_args)


def _mix_ffn_kernel(x_ref, hd_ref, wout_ref, wup_ref, wdn_ref, gpm_ref, gpf_ref, gqf_ref, y_ref):
    mix = _dot(hd_ref[...].astype(BF16), wout_ref[...])
    x1 = x_ref[...] + _rms_f32(mix, gpm_ref[...])
    h = _rms_f32(x1, gpf_ref[...]).astype(BF16)
    u = jnp.maximum(_dot(h, wup_ref[...]), 0.0)
    f = _dot((u * u).astype(BF16), wdn_ref[...])
    y_ref[...] = x1 + _rms_f32(f, gqf_ref[...])


def _mix_ffn(x, heads, w_out, w_up, w_down, g_post_mix, g_pre_ffn, g_post_ffn, tm):
    n = x.shape[0]
    tm = min(tm, n)
    assert n % tm == 0
    row = lambda w: pl.BlockSpec((tm, w), lambda i: (i, 0))
    const = lambda a: pl.BlockSpec(a.shape, lambda i: (0,) * a.ndim, pipeline_mode=pl.Buffered(1))
    consts = [w_out, w_up, w_down, g_post_mix, g_pre_ffn, g_post_ffn]
    return pl.pallas_call(
        _mix_ffn_kernel,
        grid=(n // tm,),
        in_specs=[row(D_MODEL), row(2 * HW)] + [const(a) for a in consts],
        out_specs=row(D_MODEL),
        out_shape=jax.ShapeDtypeStruct((n, D_MODEL), F32),
        name="mix_ffn",
        compiler_params=pltpu.CompilerParams(
            dimension_semantics=("arbitrary",), vmem_limit_bytes=VMEM_LIMIT),
    )(x, heads, *consts)


def _prep_weights(w_in, w_q_b, w_kv_b):
    w_in_p = jnp.pad(w_in, ((0, 0), (0, IN_COLS_PAD - IN_COLS))).astype(BF16)
    w_va_t = w_in[:, C_VA:C_VA + HW].T.astype(BF16)
    w_qb_p = jnp.pad(w_q_b, ((0, 0), (0, 0), (0, LANE - NOPE_B - ROPE_B))).reshape(Q_LORA, HW)
    w_uk = w_kv_b[..., :NOPE_B]
    w_uv = w_kv_b[..., NOPE_B:]
    w_uk_p = jnp.pad(w_uk, ((0, 0), (0, 0), (0, LANE - NOPE_B))).reshape(KV_LORA, HW)
    w_uv_f = w_uv.reshape(KV_LORA, HW)
    eye_r = jnp.eye(ROPE_B, dtype=w_in.dtype)
    per_head = []
    for h in range(H):
        lat = jnp.concatenate([w_uk[:, h, :].T, jnp.zeros((LANE - NOPE_B, KV_LORA), w_in.dtype)], 0)
        rope = jnp.concatenate([jnp.zeros((NOPE_B, ROPE_B), w_in.dtype), eye_r,
                                jnp.zeros((LANE - NOPE_B - ROPE_B, ROPE_B), w_in.dtype)], 0)
        per_head.append(jnp.concatenate([lat, rope], axis=1))
    w_abs = jnp.concatenate(per_head, axis=0)
    return (w_in_p, w_qb_p.astype(BF16), w_uk_p.astype(BF16), w_va_t, w_uv_f.T.astype(BF16),
            w_abs.astype(BF16), w_uv_f.astype(BF16))


def kernel(x_prompt, x_sample, cache_diff_k, cache_diff_v, cache_mla_latent, cache_mla_krope,
           page_table, meta_tokens, g_pre_mix, g_post_mix, g_pre_ffn, g_post_ffn, w_in, g_q_a,
           w_q_b, g_kv_a, w_kv_b, lambda_q1, lambda_k1, lambda_q2, lambda_k2, g_sub, w_out,
           w_up, w_down):
    b, s, d = x_prompt.shape
    db, t, _ = x_sample.shape
    n_pool = cache_diff_k.shape[1]
    past = page_table.shape[1] * PAGE
    assert w_in.shape[0] == 1, "single-layer trunk"
    layer = 0

    w_in_p, w_qb_p, w_uk_p, w_va_t, w_uv_t, w_abs, w_uv_f = _prep_weights(
        w_in[layer], w_q_b[layer], w_kv_b[layer])
    proj_w = (g_pre_mix, w_in_p, g_q_a, w_qb_p, g_kv_a, w_uk_p, w_va_t, w_uv_t)
    lams = (lambda_q1, lambda_k1, lambda_q2, lambda_k2)

    tm = 512
    px = _project(x_prompt.reshape(b * s, d), N_META + jnp.arange(s), tm, b, proj_w)
    pm = _project(meta_tokens.astype(x_prompt.dtype), jnp.arange(N_META), N_META, 1, proj_w)
    tms = min(tm, db * t)
    ps = _project(x_sample.reshape(db * t, d), past + (jnp.arange(tms) % t), tms, 1, proj_w)

    heads_p = _prompt_attend(px, pm, g_sub, lams, b, s, tq=512, sk=256)
    caches = (jnp.transpose(cache_diff_k[layer], (0, 2, 3, 4, 1)).reshape(n_pool, HW, PAGE),
              cache_diff_v[layer].reshape(n_pool, PAGE * H, DV),
              cache_mla_latent[layer],
              jnp.transpose(cache_mla_krope[layer], (0, 2, 1)))
    heads_s = _sample_attend(ps, caches, page_table, w_abs, w_uv_f, g_sub, lams, db, t, pp=8)

    ffn_w = (w_out[layer].astype(BF16), w_up[layer].astype(BF16), w_down[layer].astype(BF16),
             g_post_mix, g_pre_ffn, g_post_ffn)
    y_prompt = _mix_ffn(x_prompt.reshape(b * s, d), heads_p.reshape(b * s, 2 * HW), *ffn_w, tm=512)
    y_sample = _mix_ffn(x_sample.reshape(db * t, d), heads_s.reshape(db * t, 2 * HW), *ffn_w, tm=512)

    def with_meta(x_part, m_part, tail):
        m = jnp.broadcast_to(m_part[None], (b, N_META, m_part.shape[-1]))
        full = jnp.concatenate([m, x_part.reshape(b, s, -1)], axis=1)
        return full.reshape((1, b, s + N_META) + tail)

    k_shape, v_shape = (H, 2, DH_A), (H, DV)
    return (y_prompt.reshape(b, s, d), y_sample.reshape(db, t, d),
            with_meta(px[1], pm[1], k_shape), with_meta(px[3], pm[3], v_shape),
            with_meta(px[4], pm[4], (KV_LORA,)), with_meta(px[5], pm[5], (ROPE_B,)),
            ps[1].reshape((1, db, t) + k_shape), ps[3].reshape((1, db, t) + v_shape),
            ps[4].reshape(1, db, t, KV_LORA), ps[5].reshape(1, db, t, ROPE_B))
```

```python
import functools
import math

import jax
import jax.numpy as jnp
import numpy as np
from jax import lax
from jax.experimental import pallas as pl
from jax.experimental.pallas import tpu as pltpu

D_MODEL = 1024
N_META = 16
H = 4
DH_A = 64
DV = 128
ROT_A = 16
Q_LORA = 384
KV_LORA = 256
NOPE_B = 64
ROPE_B = 32
D_FF = 4096
PAGE = 128
ROPE_THETA = 500000.0
RMS_EPS = 1e-6
NEG_INF = -1e30
SCALE_A = DH_A ** -0.5
SCALE_B = (NOPE_B + ROPE_B) ** -0.5
LAM_INIT = 0.8 - 0.6 * math.exp(-0.3 * 0)

LANE = 128
ONES_ROWS = 16
HW = H * LANE
IN_COLS = 2208
IN_COLS_PAD = 2304
C_QA, C_KA, C_VA, C_QL, C_CKV, C_KR = 0, 512, 1024, 1536, 1920, 2176

VMEM_LIMIT = 48 * 1024 * 1024

F32 = jnp.float32
BF16 = jnp.bfloat16


def _rms_f32(x, g):
    return x * lax.rsqrt(jnp.mean(x * x, axis=-1, keepdims=True) + RMS_EPS) * g


def _rot(x, c, sm, sp, half):
    w = x.shape[-1]
    return x * c + pltpu.roll(x, w - half, 1) * sm + pltpu.roll(x, half, 1) * sp


def _dot(a, b):
    return jnp.dot(a, b, preferred_element_type=F32)


def _dot_nt(a, b):
    return lax.dot_general(a, b, (((1,), (1,)), ((), ())), preferred_element_type=F32)


def _project_kernel(x_ref, g_ref, win_ref, gq_ref, wqb_ref, gkv_ref, wuk_ref, wvat_ref, wuvt_ref,
                    ca_ref, sma_ref, spa_ref, cb_ref, smb_ref, spb_ref,
                    qa_o, kaf_o, kab_o, vaf_o, ckv_o, kr_o, qm_o, km_o, vat_o, vbt_o):
    h = _rms_f32(x_ref[...], g_ref[...]).astype(BF16)
    z = _dot(h, win_ref[...])
    ca, sma, spa = ca_ref[...], sma_ref[...], spa_ref[...]
    for j in range(H):
        sl = slice(j * LANE, (j + 1) * LANE)
        q = _rot(z[:, C_QA + j * LANE:C_QA + (j + 1) * LANE], ca, sma, spa, ROT_A // 2)
        k = _rot(z[:, C_KA + j * LANE:C_KA + (j + 1) * LANE], ca, sma, spa, ROT_A // 2)
        qa_o[:, sl] = (q * SCALE_A).astype(BF16)
        kaf_o[:, sl] = k
        kab_o[:, sl] = k.astype(BF16)
    vaf_o[...] = z[:, C_VA:C_VA + HW]
    vat_o[0] = _dot_nt(wvat_ref[...], h).astype(BF16)

    cb, smb, spb = cb_ref[...], smb_ref[...], spb_ref[...]
    kr = _rot(z[:, C_KR:C_KR + LANE], cb, smb, spb, ROPE_B // 2)
    kr_o[...] = kr[:, :ROPE_B]
    kr_at64 = pltpu.roll(kr, NOPE_B, 1)

    ckv = _rms_f32(z[:, C_CKV:C_CKV + KV_LORA], gkv_ref[...])
    ckv_o[...] = ckv
    ckv_b = ckv.astype(BF16)
    kn = _dot(ckv_b, wuk_ref[...])
    vbt_o[0] = _dot_nt(wuvt_ref[...], ckv_b).astype(BF16)

    ql = _rms_f32(z[:, C_QL:C_QL + Q_LORA], gq_ref[...]).astype(BF16)
    qb = _dot(ql, wqb_ref[...])
    cq = pltpu.roll(cb, NOPE_B, 1)
    smq = pltpu.roll(smb, NOPE_B, 1)
    spq = pltpu.roll(spb, NOPE_B, 1)
    for j in range(H):
        sl = slice(j * LANE, (j + 1) * LANE)
        qm_o[:, sl] = _rot(qb[:, sl], cq, smq, spq, ROPE_B // 2).astype(BF16)
        km_o[:, sl] = (kn[:, sl] + kr_at64).astype(BF16)


def _rope_tables(pos):
    pos = pos.astype(F32)[:, None]
    lane = np.arange(LANE)

    def build(period, half, active):
        idx = lane % period
        inv = 1.0 / (ROPE_THETA ** (jnp.arange(half, dtype=F32) / half))
        ang = pos * inv[None, :]
        cos = jnp.cos(ang)[:, idx % half]
        sin = jnp.sin(ang)[:, idx % half]
        first = jnp.asarray(active & (idx < half))
        second = jnp.asarray(active & (idx >= half) & (idx < 2 * half))
        c = jnp.where(first | second, cos, 1.0)
        sm = jnp.where(first, -sin, 0.0)
        sp = jnp.where(second, sin, 0.0)
        return c, sm, sp

    a = build(DH_A, ROT_A // 2, np.ones(LANE, bool))
    b = build(LANE, ROPE_B // 2, lane < ROPE_B)
    return a + b


def _project(x, pos, tm, nb, wts):
    n = x.shape[0]
    n_pos_tiles = pos.shape[0] // tm
    tiles_per_b = n // nb // tm
    tables = _rope_tables(pos)
    grid = (n // tm,)
    row = lambda w: pl.BlockSpec((tm, w), lambda i: (i, 0))
    full = lambda a: pl.BlockSpec(a.shape, lambda i: (0,) * a.ndim)
    tab = pl.BlockSpec((tm, LANE), lambda i: (i % n_pos_tiles, 0))
    colmajor = pl.BlockSpec((1, HW, tm), lambda i: (i // tiles_per_b, 0, i % tiles_per_b))
    outs = [(HW, BF16), (HW, F32), (HW, BF16), (HW, F32), (KV_LORA, F32), (ROPE_B, F32),
            (HW, BF16), (HW, BF16)]
    t_shape = jax.ShapeDtypeStruct((nb, HW, n // nb), BF16)
    return pl.pallas_call(
        _project_kernel,
        grid=grid,
        in_specs=[row(D_MODEL)] + [full(a) for a in wts] + [tab] * 6,
        out_specs=[row(w) for w, _ in outs] + [colmajor, colmajor],
        out_shape=[jax.ShapeDtypeStruct((n, w), dt) for w, dt in outs] + [t_shape, t_shape],
        name="project",
        compiler_params=pltpu.CompilerParams(
            dimension_semantics=("arbitrary",), vmem_limit_bytes=VMEM_LIMIT),
    )(x, *wts, *tables)


def _lambda(lq1, lk1, lq2, lk2):
    return (jnp.exp(jnp.sum(lq1[...] * lk1[...], keepdims=True))
            - jnp.exp(jnp.sum(lq2[...] * lk2[...], keepdims=True)) + LAM_INIT)


def _softmax_step(m_ref, l_ref, idx, s, mask, axis):
    if mask is not None:
        s = jnp.where(mask, s, NEG_INF)
    m_old = m_ref[idx]
    m_new = jnp.maximum(m_old, jnp.max(s, axis=axis, keepdims=True))
    p = jnp.exp(s - m_new)
    if mask is not None:
        p = jnp.where(mask, p, 0.0)
    corr = jnp.exp(m_old - m_new)
    m_ref[idx] = m_new
    l_ref[idx] = l_ref[idx] * corr + jnp.sum(p, axis=axis, keepdims=True)
    return p, corr


def _prompt_attend_kernel(qa_ref, qm_ref, ka_ref, km_ref, vat_ref, vbt_ref,
                          mka_ref, mkm_ref, mvat_ref, mvbt_ref,
                          gcol_ref, lq1, lk1, lq2, lk2,
                          out_ref, qw, ma, acca, mb, accb, *, tq, sk, cw):
    qi = pl.program_id(1)
    ki = pl.program_id(2)

    def update(ka, km, vat, vbt, mask):
        streams = [(h, grp, cb) for h in range(H)
                   for grp, ncb in (("a", 2 * tq // cw), ("b", tq // cw)) for cb in range(ncb)]

        def scores(st):
            h, grp, cb = st
            sl = slice(h * LANE, (h + 1) * LANE)
            cols = slice(cb * cw, (cb + 1) * cw)
            if grp == "a":
                return _dot_nt(ka[:, sl], qw[h, cols, :])
            return _dot_nt(km[:, sl], qm_ref[0, cols, sl]) * SCALE_B

        s_next = scores(streams[0])
        for i, (h, grp, cb) in enumerate(streams):
            s = s_next
            if i + 1 < len(streams):
                s_next = scores(streams[i + 1])
            sl = slice(h * LANE, (h + 1) * LANE)
            idx = (h, slice(None), slice(cb * cw, (cb + 1) * cw))
            qcol = (cb * cw) % tq
            mk = None if mask is None else mask[:, qcol:qcol + cw]
            m_ref, acc_ref, vt = (ma, acca, vat) if grp == "a" else (mb, accb, vbt)
            if mk is not None:
                s = jnp.where(mk, s, NEG_INF)
            m_old = m_ref[idx]
            m_new = jnp.maximum(m_old, jnp.max(s, axis=0, keepdims=True))
            p = jnp.exp(s - m_new)
            if mk is not None:
                p = jnp.where(mk, p, 0.0)
            m_ref[idx] = m_new
            v_ext = jnp.concatenate([vt[sl, :], jnp.ones((ONES_ROWS, vt.shape[1]), BF16)], axis=0)
            acc_ref[idx] = acc_ref[idx] * jnp.exp(m_old - m_new) + _dot(v_ext, p.astype(BF16))

    @pl.when(ki == 0)
    def _():
        lane = lax.broadcasted_iota(jnp.int32, (tq, LANE), 1)
        for h in range(H):
            q = qa_ref[0, :, h * LANE:(h + 1) * LANE]
            qw[h, :tq] = jnp.where(lane < DH_A, q, 0)
            qw[h, tq:] = jnp.where(lane >= DH_A, q, 0)
        ma[...] = jnp.full(ma.shape, NEG_INF, F32)
        mb[...] = jnp.full(mb.shape, NEG_INF, F32)
        acca[...] = jnp.zeros(acca.shape, F32)
        accb[...] = jnp.zeros(accb.shape, F32)
        update(mka_ref[...], mkm_ref[...], mvat_ref[0], mvbt_ref[0], None)

    @pl.when(ki < qi)
    def _():
        for j in range(tq // sk):
            rows = slice(j * sk, (j + 1) * sk)
            update(ka_ref[0, rows, :], km_ref[0, rows, :], vat_ref[0, :, rows], vbt_ref[0, :, rows],
                   None)

    @pl.when(ki == qi)
    def _():
        r = lax.broadcasted_iota(jnp.int32, (sk, tq), 0)
        c = lax.broadcasted_iota(jnp.int32, (sk, tq), 1)
        for j in range(tq // sk):
            rows = slice(j * sk, (j + 1) * sk)
            update(ka_ref[0, rows, :], km_ref[0, rows, :], vat_ref[0, :, rows], vbt_ref[0, :, rows],
                   r + j * sk <= c)
        lam = _lambda(lq1, lk1, lq2, lk2)
        for h in range(H):
            o12 = acca[h, :DV] / acca[h, DV:DV + 1]
            d = o12[:, :tq] - lam * o12[:, tq:]
            ms = jnp.mean(d * d, axis=0, keepdims=True)
            o = d * lax.rsqrt(ms + RMS_EPS) * gcol_ref[...] * (1.0 - LAM_INIT)
            out_ref[0, :, h * LANE:(h + 1) * LANE] = o.T.astype(out_ref.dtype)
            ob = accb[h, :DV] / accb[h, DV:DV + 1]
            out_ref[0, :, HW + h * LANE:HW + (h + 1) * LANE] = ob.T.astype(out_ref.dtype)


def _prompt_attend(px, pm, g_sub, lams, b, s, tq, sk, cw):
    qa, _, kab, _, _, _, qm, km, vat, vbt = px
    _, _, mkab, _, _, _, _, mkm, mvat, mvbt = pm
    nq = s // tq
    r3 = lambda a: a.reshape(b, s, HW)
    qspec = pl.BlockSpec((1, tq, HW), lambda bi, qi, ki: (bi, qi, 0))
    kspec = pl.BlockSpec((1, tq, HW), lambda bi, qi, ki: (bi, jnp.minimum(ki, qi), 0))
    vspec = pl.BlockSpec((1, HW, tq), lambda bi, qi, ki: (bi, 0, jnp.minimum(ki, qi)))
    full = lambda a: pl.BlockSpec(a.shape, lambda bi, qi, ki: (0,) * a.ndim)
    gcol = jnp.broadcast_to(g_sub.reshape(DV, 1), (DV, tq))
    consts = [mkab, mkm, mvat, mvbt, gcol] + list(lams)
    return pl.pallas_call(
        functools.partial(_prompt_attend_kernel, tq=tq, sk=sk, cw=cw),
        grid=(b, nq, nq),
        in_specs=[qspec, qspec, kspec, kspec, vspec, vspec] + [full(a) for a in consts],
        out_specs=pl.BlockSpec((1, tq, 2 * HW), lambda bi, qi, ki: (bi, qi, 0)),
        out_shape=jax.ShapeDtypeStruct((b, s, 2 * HW), BF16),
        name="prompt_attend",
        scratch_shapes=[pltpu.VMEM((H, 2 * tq, LANE), BF16),
                        pltpu.VMEM((H, 1, 2 * tq), F32), pltpu.VMEM((H, DV + ONES_ROWS, 2 * tq), F32),
                        pltpu.VMEM((H, 1, tq), F32), pltpu.VMEM((H, DV + ONES_ROWS, tq), F32)],
        compiler_params=pltpu.CompilerParams(
            dimension_semantics=("arbitrary", "arbitrary", "arbitrary"),
            vmem_limit_bytes=VMEM_LIMIT),
    )(r3(qa), r3(qm), r3(kab), r3(km), vat, vbt, *consts)


def _sample_attend_kernel(pt_ref, qa_ref, qm_ref, kan_ref, van_ref, cn_ref, krn_ref,
                          wabs_ref, wuv_ref, gsub_ref, lq1, lk1, lq2, lk2, *rest, pp, t):
    kt_refs, v_refs, c_refs, krt_refs = (rest[i * pp:(i + 1) * pp] for i in range(4))
    out_ref, qbd, qext, ma, la, acca, mb, lb, accb = rest[4 * pp:]
    g = pl.program_id(1)
    ra, rb = 2 * H * t, H * t
    head_a = (lax.broadcasted_iota(jnp.int32, (ra, LANE), 0) // t) % H

    def rep_rows(q, rows):
        tok = lax.broadcasted_iota(jnp.int32, (rows, HW), 0) % t
        out = jnp.broadcast_to(q[t - 1:t], (rows, HW))
        for i in range(t - 2, -1, -1):
            out = jnp.where(tok == i, jnp.broadcast_to(q[i:i + 1], (rows, HW)), out)
        return out

    @pl.when(g == 0)
    def _():
        r = lax.broadcasted_iota(jnp.int32, (ra, HW), 0)
        ln = lax.broadcasted_iota(jnp.int32, (ra, HW), 1)
        chunk = ((r // t) % H) * 2 + r // (H * t)
        qbd[...] = jnp.where(ln // DH_A == chunk, rep_rows(qa_ref[0], ra), 0.0).astype(BF16)
        r = lax.broadcasted_iota(jnp.int32, (rb, HW), 0)
        ln = lax.broadcasted_iota(jnp.int32, (rb, HW), 1)
        q = jnp.where(ln // LANE == r // t, rep_rows(qm_ref[0], rb), 0.0).astype(BF16)
        qext[...] = _dot(q, wabs_ref[...]).astype(BF16)
        ma[...] = jnp.full(ma.shape, NEG_INF, F32)
        mb[...] = jnp.full(mb.shape, NEG_INF, F32)
        la[...] = jnp.zeros(la.shape, F32)
        lb[...] = jnp.zeros(lb.shape, F32)
        acca[...] = jnp.zeros(acca.shape, F32)
        accb[...] = jnp.zeros(accb.shape, F32)

    def accumulate(p_a, corr_a, v_of_head, p_b, corr_b, c):
        p_a = p_a.astype(BF16)
        new = jnp.zeros((ra, DV), F32)
        for h in range(H):
            new = jnp.where(head_a == h, _dot(p_a, v_of_head(h)), new)
        acca[0] = acca[0] * corr_a + new
        accb[0] = accb[0] * corr_b + _dot(p_b.astype(BF16), c)

    qe = qext[...]
    kt = jnp.concatenate([r[0].astype(BF16) for r in kt_refs], axis=1)
    krt = jnp.concatenate([r[0].astype(BF16) for r in krt_refs], axis=1)
    c = jnp.concatenate([r[0].astype(BF16) for r in c_refs], axis=0)
    s_b = (_dot_nt(qe[:, :KV_LORA], c) + _dot(qe[:, KV_LORA:], krt)) * SCALE_B
    p_a, corr_a = _softmax_step(ma, la, 0, _dot(qbd[...], kt), None, 1)
    p_b, corr_b = _softmax_step(mb, lb, 0, s_b, None, 1)

    def page_values(h):
        return jnp.concatenate([r[0, pl.ds(h, PAGE, stride=H), :].astype(BF16) for r in v_refs],
                               axis=0)

    accumulate(p_a, corr_a, page_values, p_b, corr_b, c)

    @pl.when(g == pl.num_programs(1) - 1)
    def _():
        nk = 16

        def pad_rows(x):
            r = lax.broadcasted_iota(jnp.int32, (nk, x.shape[1]), 0)
            out = jnp.zeros((nk, x.shape[1]), x.dtype)
            for i in range(t):
                out = jnp.where(r == i, jnp.broadcast_to(x[i:i + 1], out.shape), out)
            return out.astype(BF16)

        kn, vn, cn, krn = (pad_rows(r[0]) for r in (kan_ref, van_ref, cn_ref, krn_ref))
        causal_a = (lax.broadcasted_iota(jnp.int32, (ra, nk), 1)
                    <= lax.broadcasted_iota(jnp.int32, (ra, nk), 0) % t)
        causal_b = (lax.broadcasted_iota(jnp.int32, (rb, nk), 1)
                    <= lax.broadcasted_iota(jnp.int32, (rb, nk), 0) % t)
        s_n = (_dot_nt(qe[:, :KV_LORA], cn) + _dot_nt(qe[:, KV_LORA:], krn)) * SCALE_B
        pn_a, cr_a = _softmax_step(ma, la, 0, _dot_nt(qbd[...], kn), causal_a, 1)
        pn_b, cr_b = _softmax_step(mb, lb, 0, s_n, causal_b, 1)
        accumulate(pn_a, cr_a, lambda h: vn[:, h * LANE:(h + 1) * LANE], pn_b, cr_b, cn)

        lam = _lambda(lq1, lk1, lq2, lk2)
        oa = acca[0] / la[0]
        o_a = _rms_f32(oa[:rb] - lam * oa[rb:], gsub_ref[...]) * (1.0 - LAM_INIT)
        o_b = _dot((accb[0] / lb[0]).astype(BF16), wuv_ref[...])
        for h in range(H):
            rows = slice(h * t, (h + 1) * t)
            out_ref[0, :, h * LANE:(h + 1) * LANE] = o_a[rows, :]
            out_ref[0, :, HW + h * LANE:HW + (h + 1) * LANE] = o_b[rows, h * LANE:(h + 1) * LANE]


def _sample_attend(ps, caches, page_table, wabs, wuv, g_sub, lams, b, t, pp):
    qa, kaf, _, vaf, ckv, kr, qm, _, _, _ = ps
    n_pages = page_table.shape[1]
    r3 = lambda a: a.reshape(b, t, a.shape[-1])
    tok = lambda w: pl.BlockSpec((1, t, w), lambda bi, g, pt: (bi, 0, 0))
    full = lambda a: pl.BlockSpec(a.shape, lambda bi, g, pt: (0,) * a.ndim)

    def page(shape, p):
        return pl.BlockSpec((1,) + shape, lambda bi, g, pt: (pt[bi, g * pp + p], 0, 0))

    consts = [wabs, wuv, g_sub] + list(lams)
    cache_specs, cache_args = [], []
    for a in caches:
        for p in range(pp):
            cache_specs.append(page(a.shape[1:], p))
            cache_args.append(a)
    ra, rb = 2 * t * H, t * H
    grid_spec = pltpu.PrefetchScalarGridSpec(
        num_scalar_prefetch=1,
        grid=(b, n_pages // pp),
        in_specs=[tok(HW), tok(HW), tok(HW), tok(HW), tok(KV_LORA), tok(ROPE_B)]
                 + [full(a) for a in consts] + cache_specs,
        out_specs=pl.BlockSpec((1, t, 2 * HW), lambda bi, g, pt: (bi, 0, 0)),
        scratch_shapes=[pltpu.VMEM((ra, HW), BF16), pltpu.VMEM((rb, KV_LORA + ROPE_B), BF16),
                        pltpu.VMEM((1, ra, 1), F32), pltpu.VMEM((1, ra, 1), F32),
                        pltpu.VMEM((1, ra, DV), F32),
                        pltpu.VMEM((1, rb, 1), F32), pltpu.VMEM((1, rb, 1), F32),
                        pltpu.VMEM((1, rb, KV_LORA), F32)])
    return pl.pallas_call(
        functools.partial(_sample_attend_kernel, pp=pp, t=t),
        grid_spec=grid_spec,
        out_shape=jax.ShapeDtypeStruct((b, t, 2 * HW), F32),
        name="sample_attend",
        compiler_params=pltpu.CompilerParams(
            dimension_semantics=("arbitrary", "arbitrary"), vmem_limit_bytes=VMEM_LIMIT),
    )(page_table, r3(qa.astype(F32)), r3(qm.astype(F32)), r3(kaf), r3(vaf), r3(ckv), r3(kr),
      *consts, *cache_args)


def _mix_ffn_kernel(x_ref, hd_ref, wout_ref, wup_ref, wdn_ref, gpm_ref, gpf_ref, gqf_ref, y_ref):
    mix = _dot(hd_ref[...].astype(BF16), wout_ref[...])
    x1 = x_ref[...] + _rms_f32(mix, gpm_ref[...])
    h = _rms_f32(x1, gpf_ref[...]).astype(BF16)
    u = jnp.maximum(_dot(h, wup_ref[...]), 0.0)
    f = _dot((u * u).astype(BF16), wdn_ref[...])
    y_ref[...] = x1 + _rms_f32(f, gqf_ref[...])


def _mix_ffn(x, heads, w_out, w_up, w_down, g_post_mix, g_pre_ffn, g_post_ffn, tm):
    n = x.shape[0]
    tm = min(tm, n)
    assert n % tm == 0
    row = lambda w: pl.BlockSpec((tm, w), lambda i: (i, 0))
    const = lambda a: pl.BlockSpec(a.shape, lambda i: (0,) * a.ndim, pipeline_mode=pl.Buffered(1))
    consts = [w_out, w_up, w_down, g_post_mix, g_pre_ffn, g_post_ffn]
    return pl.pallas_call(
        _mix_ffn_kernel,
        grid=(n // tm,),
        in_specs=[row(D_MODEL), row(2 * HW)] + [const(a) for a in consts],
        out_specs=row(D_MODEL),
        out_shape=jax.ShapeDtypeStruct((n, D_MODEL), F32),
        name="mix_ffn",
        compiler_params=pltpu.CompilerParams(
            dimension_semantics=("arbitrary",), vmem_limit_bytes=VMEM_LIMIT),
    )(x, heads, *consts)


def _prep_weights(w_in, w_q_b, w_kv_b):
    w_in_p = jnp.pad(w_in, ((0, 0), (0, IN_COLS_PAD - IN_COLS))).astype(BF16)
    w_va_t = w_in[:, C_VA:C_VA + HW].T.astype(BF16)
    w_qb_p = jnp.pad(w_q_b, ((0, 0), (0, 0), (0, LANE - NOPE_B - ROPE_B))).reshape(Q_LORA, HW)
    w_uk = w_kv_b[..., :NOPE_B]
    w_uv = w_kv_b[..., NOPE_B:]
    w_uk_p = jnp.pad(w_uk, ((0, 0), (0, 0), (0, LANE - NOPE_B))).reshape(KV_LORA, HW)
    w_uv_f = w_uv.reshape(KV_LORA, HW)
    eye_r = jnp.eye(ROPE_B, dtype=w_in.dtype)
    per_head = []
    for h in range(H):
        lat = jnp.concatenate([w_uk[:, h, :].T, jnp.zeros((LANE - NOPE_B, KV_LORA), w_in.dtype)], 0)
        rope = jnp.concatenate([jnp.zeros((NOPE_B, ROPE_B), w_in.dtype), eye_r,
                                jnp.zeros((LANE - NOPE_B - ROPE_B, ROPE_B), w_in.dtype)], 0)
        per_head.append(jnp.concatenate([lat, rope], axis=1))
    w_abs = jnp.concatenate(per_head, axis=0)
    return (w_in_p, w_qb_p.astype(BF16), w_uk_p.astype(BF16), w_va_t, w_uv_f.T.astype(BF16),
            w_abs.astype(BF16), w_uv_f.astype(BF16))


def kernel(x_prompt, x_sample, cache_diff_k, cache_diff_v, cache_mla_latent, cache_mla_krope,
           page_table, meta_tokens, g_pre_mix, g_post_mix, g_pre_ffn, g_post_ffn, w_in, g_q_a,
           w_q_b, g_kv_a, w_kv_b, lambda_q1, lambda_k1, lambda_q2, lambda_k2, g_sub, w_out,
           w_up, w_down):
    b, s, d = x_prompt.shape
    db, t, _ = x_sample.shape
    n_pool = cache_diff_k.shape[1]
    past = page_table.shape[1] * PAGE
    assert w_in.shape[0] == 1, "single-layer trunk"
    layer = 0

    w_in_p, w_qb_p, w_uk_p, w_va_t, w_uv_t, w_abs, w_uv_f = _prep_weights(
        w_in[layer], w_q_b[layer], w_kv_b[layer])
    proj_w = (g_pre_mix, w_in_p, g_q_a, w_qb_p, g_kv_a, w_uk_p, w_va_t, w_uv_t)
    lams = (lambda_q1, lambda_k1, lambda_q2, lambda_k2)

    tm = 512
    px = _project(x_prompt.reshape(b * s, d), N_META + jnp.arange(s), tm, b, proj_w)
    pm = _project(meta_tokens.astype(x_prompt.dtype), jnp.arange(N_META), N_META, 1, proj_w)
    tms = min(tm, db * t)
    ps = _project(x_sample.reshape(db * t, d), past + (jnp.arange(tms) % t), tms, 1, proj_w)

    heads_p = _prompt_attend(px, pm, g_sub, lams, b, s, tq=512, sk=512, cw=512)
    caches = (jnp.transpose(cache_diff_k[layer], (0, 2, 3, 4, 1)).reshape(n_pool, HW, PAGE),
              cache_diff_v[layer].reshape(n_pool, PAGE * H, DV),
              cache_mla_latent[layer],
              jnp.transpose(cache_mla_krope[layer], (0, 2, 1)))
    heads_s = _sample_attend(ps, caches, page_table, w_abs, w_uv_f, g_sub, lams, db, t, pp=8)

    ffn_w = (w_out[layer].astype(BF16), w_up[layer].astype(BF16), w_down[layer].astype(BF16),
             g_post_mix, g_pre_ffn, g_post_ffn)
    y_prompt = _mix_ffn(x_prompt.reshape(b * s, d), heads_p.reshape(b * s, 2 * HW), *ffn_w, tm=512)
    y_sample = _mix_ffn(x_sample.reshape(db * t, d), heads_s.reshape(db * t, 2 * HW), *ffn_w, tm=512)

    def with_meta(x_part, m_part, tail):
        m = jnp.broadcast_to(m_part[None], (b, N_META, m_part.shape[-1]))
        full = jnp.concatenate([m, x_part.reshape(b, s, -1)], axis=1)
        return full.reshape((1, b, s + N_META) + tail)

    k_shape, v_shape = (H, 2, DH_A), (H, DV)
    return (y_prompt.reshape(b, s, d), y_sample.reshape(db, t, d),
            with_meta(px[1], pm[1], k_shape), with_meta(px[3], pm[3], v_shape),
            with_meta(px[4], pm[4], (KV_LORA,)), with_meta(px[5], pm[5], (ROPE_B,)),
            ps[1].reshape((1, db, t) + k_shape), ps[3].reshape((1, db, t) + v_shape),
            ps[4].reshape(1, db, t, KV_LORA), ps[5].reshape(1, db, t, ROPE_B))
```

```python
import functools
import math

import jax
import jax.numpy as jnp
import numpy as np
from jax import lax
from jax.experimental import pallas as pl
from jax.experimental.pallas import tpu as pltpu

D_MODEL = 1024
N_META = 16
H = 4
DH_A = 64
DV = 128
ROT_A = 16
Q_LORA = 384
KV_LORA = 256
NOPE_B = 64
ROPE_B = 32
D_FF = 4096
PAGE = 128
ROPE_THETA = 500000.0
RMS_EPS = 1e-6
NEG_INF = -1e30
SCALE_A = DH_A ** -0.5
SCALE_B = (NOPE_B + ROPE_B) ** -0.5
LAM_INIT = 0.8 - 0.6 * math.exp(-0.3 * 0)

LANE = 128
CHAIN_PAGES = 4
ONES_ROWS = 16
HW = H * LANE
IN_COLS = 2208
IN_COLS_PAD = 2304
C_QA, C_KA, C_VA, C_QL, C_CKV, C_KR = 0, 512, 1024, 1536, 1920, 2176

VMEM_LIMIT = 48 * 1024 * 1024

F32 = jnp.float32
BF16 = jnp.bfloat16


def _rms_f32(x, g):
    return x * lax.rsqrt(jnp.mean(x * x, axis=-1, keepdims=True) + RMS_EPS) * g


def _rot(x, c, sm, sp, half):
    w = x.shape[-1]
    return x * c + pltpu.roll(x, w - half, 1) * sm + pltpu.roll(x, half, 1) * sp


def _dot(a, b):
    return jnp.dot(a, b, preferred_element_type=F32)


def _dot_nt(a, b):
    return lax.dot_general(a, b, (((1,), (1,)), ((), ())), preferred_element_type=F32)


def _project_kernel(x_ref, g_ref, win_ref, gq_ref, wqb_ref, gkv_ref, wuk_ref, wvat_ref, wuvt_ref,
                    ca_ref, sma_ref, spa_ref, cb_ref, smb_ref, spb_ref,
                    qa_o, kaf_o, kab_o, vaf_o, ckv_o, kr_o, qm_o, km_o, vat_o, vbt_o):
    h = _rms_f32(x_ref[...], g_ref[...]).astype(BF16)
    z = _dot(h, win_ref[...])
    ca, sma, spa = ca_ref[...], sma_ref[...], spa_ref[...]
    for j in range(H):
        sl = slice(j * LANE, (j + 1) * LANE)
        q = _rot(z[:, C_QA + j * LANE:C_QA + (j + 1) * LANE], ca, sma, spa, ROT_A // 2)
        k = _rot(z[:, C_KA + j * LANE:C_KA + (j + 1) * LANE], ca, sma, spa, ROT_A // 2)
        qa_o[:, sl] = (q * SCALE_A).astype(BF16)
        kaf_o[:, sl] = k
        kab_o[:, sl] = k.astype(BF16)
    vaf_o[...] = z[:, C_VA:C_VA + HW]
    vat_o[0] = _dot_nt(wvat_ref[...], h).astype(BF16)

    cb, smb, spb = cb_ref[...], smb_ref[...], spb_ref[...]
    kr = _rot(z[:, C_KR:C_KR + LANE], cb, smb, spb, ROPE_B // 2)
    kr_o[...] = kr[:, :ROPE_B]
    kr_at64 = pltpu.roll(kr, NOPE_B, 1)

    ckv = _rms_f32(z[:, C_CKV:C_CKV + KV_LORA], gkv_ref[...])
    ckv_o[...] = ckv
    ckv_b = ckv.astype(BF16)
    kn = _dot(ckv_b, wuk_ref[...])
    vbt_o[0] = _dot_nt(wuvt_ref[...], ckv_b).astype(BF16)

    ql = _rms_f32(z[:, C_QL:C_QL + Q_LORA], gq_ref[...]).astype(BF16)
    qb = _dot(ql, wqb_ref[...])
    cq = pltpu.roll(cb, NOPE_B, 1)
    smq = pltpu.roll(smb, NOPE_B, 1)
    spq = pltpu.roll(spb, NOPE_B, 1)
    for j in range(H):
        sl = slice(j * LANE, (j + 1) * LANE)
        qm_o[:, sl] = _rot(qb[:, sl], cq, smq, spq, ROPE_B // 2).astype(BF16)
        km_o[:, sl] = (kn[:, sl] + kr_at64).astype(BF16)


def _rope_tables(pos):
    pos = pos.astype(F32)[:, None]
    lane = np.arange(LANE)

    def build(period, half, active):
        idx = lane % period
        inv = 1.0 / (ROPE_THETA ** (jnp.arange(half, dtype=F32) / half))
        ang = pos * inv[None, :]
        cos = jnp.cos(ang)[:, idx % half]
        sin = jnp.sin(ang)[:, idx % half]
        first = jnp.asarray(active & (idx < half))
        second = jnp.asarray(active & (idx >= half) & (idx < 2 * half))
        c = jnp.where(first | second, cos, 1.0)
        sm = jnp.where(first, -sin, 0.0)
        sp = jnp.where(second, sin, 0.0)
        return c, sm, sp

    a = build(DH_A, ROT_A // 2, np.ones(LANE, bool))
    b = build(LANE, ROPE_B // 2, lane < ROPE_B)
    return a + b


def _project(x, pos, tm, nb, wts):
    n = x.shape[0]
    n_pos_tiles = pos.shape[0] // tm
    tiles_per_b = n // nb // tm
    tables = _rope_tables(pos)
    grid = (n // tm,)
    row = lambda w: pl.BlockSpec((tm, w), lambda i: (i, 0))
    full = lambda a: pl.BlockSpec(a.shape, lambda i: (0,) * a.ndim)
    tab = pl.BlockSpec((tm, LANE), lambda i: (i % n_pos_tiles, 0))
    colmajor = pl.BlockSpec((1, HW, tm), lambda i: (i // tiles_per_b, 0, i % tiles_per_b))
    outs = [(HW, BF16), (HW, F32), (HW, BF16), (HW, F32), (KV_LORA, F32), (ROPE_B, F32),
            (HW, BF16), (HW, BF16)]
    t_shape = jax.ShapeDtypeStruct((nb, HW, n // nb), BF16)
    return pl.pallas_call(
        _project_kernel,
        grid=grid,
        in_specs=[row(D_MODEL)] + [full(a) for a in wts] + [tab] * 6,
        out_specs=[row(w) for w, _ in outs] + [colmajor, colmajor],
        out_shape=[jax.ShapeDtypeStruct((n, w), dt) for w, dt in outs] + [t_shape, t_shape],
        name="project",
        compiler_params=pltpu.CompilerParams(
            dimension_semantics=("arbitrary",), vmem_limit_bytes=VMEM_LIMIT),
    )(x, *wts, *tables)


def _lambda(lq1, lk1, lq2, lk2):
    return (jnp.exp(jnp.sum(lq1[...] * lk1[...], keepdims=True))
            - jnp.exp(jnp.sum(lq2[...] * lk2[...], keepdims=True)) + LAM_INIT)


def _prompt_step(qi, ki, qa_ref, qm_ref, ka_ref, km_ref, vat_ref, vbt_ref,
                 mka_ref, mkm_ref, mvat_ref, mvbt_ref, gcol_ref, lq1, lk1, lq2, lk2,
                 out_ref, qw, ma, acca, mb, accb, *, tq, sk, cw):
    def update(ka, km, vat, vbt, mask):
        streams = [(h, grp, cb) for h in range(H)
                   for grp, ncb in (("a", 2 * tq // cw), ("b", tq // cw)) for cb in range(ncb)]

        def scores(st):
            h, grp, cb = st
            sl = slice(h * LANE, (h + 1) * LANE)
            cols = slice(cb * cw, (cb + 1) * cw)
            if grp == "a":
                return _dot_nt(ka[:, sl], qw[h, cols, :])
            return _dot_nt(km[:, sl], qm_ref[0, cols, sl]) * SCALE_B

        s_next = scores(streams[0])
        for i, (h, grp, cb) in enumerate(streams):
            s = s_next
            if i + 1 < len(streams):
                s_next = scores(streams[i + 1])
            sl = slice(h * LANE, (h + 1) * LANE)
            idx = (h, slice(None), slice(cb * cw, (cb + 1) * cw))
            qcol = (cb * cw) % tq
            mk = None if mask is None else mask[:, qcol:qcol + cw]
            m_ref, acc_ref, vt = (ma, acca, vat) if grp == "a" else (mb, accb, vbt)
            if mk is not None:
                s = jnp.where(mk, s, NEG_INF)
            m_old = m_ref[idx]
            m_new = jnp.maximum(m_old, jnp.max(s, axis=0, keepdims=True))
            p = jnp.exp(s - m_new)
            if mk is not None:
                p = jnp.where(mk, p, 0.0)
            m_ref[idx] = m_new
            v_ext = jnp.concatenate([vt[sl, :], jnp.ones((ONES_ROWS, vt.shape[1]), BF16)], axis=0)
            acc_ref[idx] = acc_ref[idx] * jnp.exp(m_old - m_new) + _dot(v_ext, p.astype(BF16))

    @pl.when(ki == 0)
    def _():
        lane = lax.broadcasted_iota(jnp.int32, (tq, LANE), 1)
        for h in range(H):
            q = qa_ref[0, :, h * LANE:(h + 1) * LANE]
            qw[h, :tq] = jnp.where(lane < DH_A, q, 0)
            qw[h, tq:] = jnp.where(lane >= DH_A, q, 0)
        ma[...] = jnp.full(ma.shape, NEG_INF, F32)
        mb[...] = jnp.full(mb.shape, NEG_INF, F32)
        acca[...] = jnp.zeros(acca.shape, F32)
        accb[...] = jnp.zeros(accb.shape, F32)
        update(mka_ref[...], mkm_ref[...], mvat_ref[0], mvbt_ref[0], None)

    @pl.when(ki < qi)
    def _():
        for j in range(tq // sk):
            rows = slice(j * sk, (j + 1) * sk)
            update(ka_ref[0, rows, :], km_ref[0, rows, :], vat_ref[0, :, rows], vbt_ref[0, :, rows],
                   None)

    @pl.when(ki == qi)
    def _():
        r = lax.broadcasted_iota(jnp.int32, (sk, tq), 0)
        c = lax.broadcasted_iota(jnp.int32, (sk, tq), 1)
        for j in range(tq // sk):
            rows = slice(j * sk, (j + 1) * sk)
            update(ka_ref[0, rows, :], km_ref[0, rows, :], vat_ref[0, :, rows], vbt_ref[0, :, rows],
                   r + j * sk <= c)
        lam = _lambda(lq1, lk1, lq2, lk2)
        for h in range(H):
            o12 = acca[h, :DV] / acca[h, DV:DV + 1]
            d = o12[:, :tq] - lam * o12[:, tq:]
            ms = jnp.mean(d * d, axis=0, keepdims=True)
            o = d * lax.rsqrt(ms + RMS_EPS) * gcol_ref[...] * (1.0 - LAM_INIT)
            out_ref[0, :, h * LANE:(h + 1) * LANE] = o.T.astype(out_ref.dtype)
            ob = accb[h, :DV] / accb[h, DV:DV + 1]
            out_ref[0, :, HW + h * LANE:HW + (h + 1) * LANE] = ob.T.astype(out_ref.dtype)


def _decode_steps(first, pt_ref, qa_ref, qm_ref, kan_ref, van_ref, cn_ref, krn_ref,
                  wabs_ref, wuv_ref, gsub_ref, lq1, lk1, lq2, lk2,
                  kt_hbm, v_hbm, c_hbm, krt_hbm,
                  out_ref, kt_buf, v_buf, c_buf, krt_buf, sem,
                  qbd, qext, ma, la, acca, mb, lb, accb, *, pp, t, sub, n_groups, total):
    last = first + sub - 1
    streams = ((kt_hbm, kt_buf), (v_hbm, v_buf), (c_hbm, c_buf), (krt_hbm, krt_buf))

    def page_copy(kind, page, to_slot, p):
        hbm, buf = streams[kind]
        return pltpu.make_async_copy(hbm.at[page], buf.at[to_slot, p], sem.at[to_slot, kind])

    def fetch(target, to_slot):
        seq, grp = target // n_groups, target % n_groups
        for p in range(pp):
            page = pt_ref[seq, grp * pp + p]
            for kind in range(len(streams)):
                page_copy(kind, page, to_slot, p).start()

    def drain(of_slot):
        for p in range(pp):
            for kind in range(len(streams)):
                page_copy(kind, 0, of_slot, p).wait()

    @pl.when(first == 0)
    def _():
        for slot in range(sub):
            fetch(slot, slot)

    ra, rb = 2 * H * t, H * t
    head_a = (lax.broadcasted_iota(jnp.int32, (ra, LANE), 0) // t) % H

    def rep_rows(q, rows):
        tok = lax.broadcasted_iota(jnp.int32, (rows, HW), 0) % t
        out = jnp.broadcast_to(q[t - 1:t], (rows, HW))
        for i in range(t - 2, -1, -1):
            out = jnp.where(tok == i, jnp.broadcast_to(q[i:i + 1], (rows, HW)), out)
        return out

    @pl.when(first % n_groups == 0)
    def _():
        r = lax.broadcasted_iota(jnp.int32, (ra, HW), 0)
        ln = lax.broadcasted_iota(jnp.int32, (ra, HW), 1)
        chunk = ((r // t) % H) * 2 + r // (H * t)
        qbd[...] = jnp.where(ln // DH_A == chunk, rep_rows(qa_ref[0], ra), 0.0).astype(BF16)
        r = lax.broadcasted_iota(jnp.int32, (rb, HW), 0)
        ln = lax.broadcasted_iota(jnp.int32, (rb, HW), 1)
        q = jnp.where(ln // LANE == r // t, rep_rows(qm_ref[0], rb), 0.0).astype(BF16)
        qext[...] = _dot(q, wabs_ref[...]).astype(BF16)
        ma[...] = jnp.full(ma.shape, NEG_INF, F32)
        mb[...] = jnp.full(mb.shape, NEG_INF, F32)
        la[...] = jnp.zeros(la.shape, F32)
        lb[...] = jnp.zeros(lb.shape, F32)
        acca[...] = jnp.zeros(acca.shape, F32)
        accb[...] = jnp.zeros(accb.shape, F32)

    def pv_a(p, v_of_head):
        p = p.astype(BF16)
        new = jnp.zeros((ra, DV), F32)
        for h in range(0, H, 2):
            pv = _dot(p, jnp.concatenate([v_of_head(h), v_of_head(h + 1)], axis=1))
            new = jnp.where(head_a == h, pv[:, :DV], jnp.where(head_a == h + 1, pv[:, DV:], new))
        return new

    def advance(state, s, mask, pv):
        m_old, l_old, acc = state
        if mask is not None:
            s = jnp.where(mask, s, NEG_INF)
        m_new = jnp.maximum(m_old, jnp.max(s, axis=1, keepdims=True))
        p = jnp.exp(s - m_new)
        if mask is not None:
            p = jnp.where(mask, p, 0.0)
        corr = jnp.exp(m_old - m_new)
        return m_new, l_old * corr + jnp.sum(p, axis=1, keepdims=True), acc * corr + pv(p)

    qe = qext[...]
    st_a = (ma[0], la[0], acca[0])
    st_b = (mb[0], lb[0], accb[0])
    for slot in range(sub):
        drain(slot)
    chains = []
    for slot in range(sub):
        for first_page in range(0, pp, CHAIN_PAGES):
            pages = range(first_page, first_page + CHAIN_PAGES)
            kt = jnp.concatenate([kt_buf[slot, p].astype(BF16) for p in pages], axis=1)
            krt = jnp.concatenate([krt_buf[slot, p].astype(BF16) for p in pages], axis=1)
            c = jnp.concatenate([c_buf[slot, p].astype(BF16) for p in pages], axis=0)
            s_a = _dot(qbd[...], kt)
            s_b = (_dot_nt(qe[:, :KV_LORA], c) + _dot(qe[:, KV_LORA:], krt)) * SCALE_B
            chains.append((slot, pages, s_a, s_b, c))
    for slot, pages, s_a, s_b, c in chains:
        def page_values(h, slot=slot, pages=pages):
            return jnp.concatenate([v_buf[slot, p, pl.ds(h, PAGE, stride=H), :].astype(BF16)
                                    for p in pages], axis=0)

        st_a = advance(st_a, s_a, None, lambda p: pv_a(p, page_values))
        st_b = advance(st_b, s_b, None, lambda p, c=c: _dot(p.astype(BF16), c))
        if pages[-1] == pp - 1:
            fetch((first + sub + slot) % total, slot)
    ma[0], la[0], acca[0] = st_a
    mb[0], lb[0], accb[0] = st_b

    @pl.when(last % n_groups == n_groups - 1)
    def _():
        nk = 16

        def pad_rows(x):
            r = lax.broadcasted_iota(jnp.int32, (nk, x.shape[1]), 0)
            out = jnp.zeros((nk, x.shape[1]), x.dtype)
            for i in range(t):
                out = jnp.where(r == i, jnp.broadcast_to(x[i:i + 1], out.shape), out)
            return out.astype(BF16)

        kn, vn, cn, krn = (pad_rows(r[0]) for r in (kan_ref, van_ref, cn_ref, krn_ref))
        causal_a = (lax.broadcasted_iota(jnp.int32, (ra, nk), 1)
                    <= lax.broadcasted_iota(jnp.int32, (ra, nk), 0) % t)
        causal_b = (lax.broadcasted_iota(jnp.int32, (rb, nk), 1)
                    <= lax.broadcasted_iota(jnp.int32, (rb, nk), 0) % t)
        s_n = (_dot_nt(qe[:, :KV_LORA], cn) + _dot_nt(qe[:, KV_LORA:], krn)) * SCALE_B
        _, l_a, acc_a = advance(st_a, _dot_nt(qbd[...], kn), causal_a,
                                lambda p: pv_a(p, lambda h: vn[:, h * LANE:(h + 1) * LANE]))
        _, l_b, acc_b = advance(st_b, s_n, causal_b, lambda p: _dot(p.astype(BF16), cn))

        lam = _lambda(lq1, lk1, lq2, lk2)
        oa = acc_a / l_a
        o_a = _rms_f32(oa[:rb] - lam * oa[rb:], gsub_ref[...]) * (1.0 - LAM_INIT)
        o_b = _dot((acc_b / l_b).astype(BF16), wuv_ref[...])
        for h in range(H):
            rows = slice(h * t, (h + 1) * t)
            out_ref[0, :, h * LANE:(h + 1) * LANE] = o_a[rows, :]
            out_ref[0, :, HW + h * LANE:HW + (h + 1) * LANE] = o_b[rows, h * LANE:(h + 1) * LANE]

    @pl.when(last == total - 1)
    def _():
        for slot in range(sub):
            drain(slot)


N_PROMPT_IN, N_PROMPT_SCRATCH = 15, 5
N_DECODE_IN, N_DECODE_SCRATCH = 17, 13


def _attend_kernel(pt_ref, *refs, tq, sk, cw, pp, t, sub, n_groups, total):
    prompt_in, refs = refs[:N_PROMPT_IN], refs[N_PROMPT_IN:]
    decode_in, refs = refs[:N_DECODE_IN], refs[N_DECODE_IN:]
    (out_p, out_s), refs = refs[:2], refs[2:]
    prompt_scr, decode_scr = refs[:N_PROMPT_SCRATCH], refs[N_PROMPT_SCRATCH:]
    assert len(decode_scr) == N_DECODE_SCRATCH
    qi, ki = pl.program_id(1), pl.program_id(2)
    step = (pl.program_id(0) * pl.num_programs(1) + qi) * pl.num_programs(2) + ki
    _decode_steps(step * sub, pt_ref, *decode_in, out_s, *decode_scr,
                  pp=pp, t=t, sub=sub, n_groups=n_groups, total=total)
    _prompt_step(qi, ki, *prompt_in, out_p, *prompt_scr, tq=tq, sk=sk, cw=cw)


def _attend(px, pm, ps, caches, page_table, wabs, wuv, g_sub, lams, b, s, db, t, tq, sk, cw, pp):
    qa, _, kab, _, _, _, qm, km, vat, vbt = px
    _, _, mkab, _, _, _, _, mkm, mvat, mvbt = pm
    sqa, skaf, _, svaf, sckv, skr, sqm, _, _, _ = ps
    nq = s // tq
    n_grid = b * nq * nq
    n_groups = page_table.shape[1] // pp
    total = db * n_groups
    sub = total // n_grid
    assert sub * n_grid == total and n_groups % sub == 0
    step_of = lambda bi, qi, ki: (bi * nq + qi) * nq + ki

    r3 = lambda a: a.reshape(b, s, HW)
    qspec = pl.BlockSpec((1, tq, HW), lambda bi, qi, ki, pt: (bi, qi, 0))
    kspec = pl.BlockSpec((1, tq, HW), lambda bi, qi, ki, pt: (bi, jnp.minimum(ki, qi), 0))
    vspec = pl.BlockSpec((1, HW, tq), lambda bi, qi, ki, pt: (bi, 0, jnp.minimum(ki, qi)))
    full = lambda a: pl.BlockSpec(a.shape, lambda bi, qi, ki, pt: (0,) * a.ndim)
    seq3 = lambda a: a.reshape(db, t, a.shape[-1])
    tok = lambda w: pl.BlockSpec((1, t, w),
                                 lambda bi, qi, ki, pt: (step_of(bi, qi, ki) * sub // n_groups, 0, 0))
    gcol = jnp.broadcast_to(g_sub.reshape(DV, 1), (DV, tq))
    prompt_args = [r3(qa), r3(qm), r3(kab), r3(km), vat, vbt, mkab, mkm, mvat, mvbt, gcol] + list(lams)
    prompt_specs = [qspec, qspec, kspec, kspec, vspec, vspec] + [full(a) for a in prompt_args[6:]]
    decode_consts = [wabs, wuv, g_sub] + list(lams)
    decode_args = ([seq3(sqa.astype(F32)), seq3(sqm.astype(F32)), seq3(skaf), seq3(svaf), seq3(sckv),
                    seq3(skr)] + decode_consts + list(caches))
    decode_specs = ([tok(HW), tok(HW), tok(HW), tok(HW), tok(KV_LORA), tok(ROPE_B)]
                    + [full(a) for a in decode_consts]
                    + [pl.BlockSpec(memory_space=pl.ANY)] * len(caches))
    assert len(prompt_args) == N_PROMPT_IN and len(decode_args) == N_DECODE_IN
    ra, rb = 2 * t * H, t * H
    prompt_scratch = [pltpu.VMEM((H, 2 * tq, LANE), BF16),
                      pltpu.VMEM((H, 1, 2 * tq), F32), pltpu.VMEM((H, DV + ONES_ROWS, 2 * tq), F32),
                      pltpu.VMEM((H, 1, tq), F32), pltpu.VMEM((H, DV + ONES_ROWS, tq), F32)]
    decode_scratch = ([pltpu.VMEM((sub, pp) + a.shape[1:], a.dtype) for a in caches]
                      + [pltpu.SemaphoreType.DMA((sub, len(caches))),
                         pltpu.VMEM((ra, HW), BF16), pltpu.VMEM((rb, KV_LORA + ROPE_B), BF16),
                         pltpu.VMEM((1, ra, 1), F32), pltpu.VMEM((1, ra, 1), F32),
                         pltpu.VMEM((1, ra, DV), F32),
                         pltpu.VMEM((1, rb, 1), F32), pltpu.VMEM((1, rb, 1), F32),
                         pltpu.VMEM((1, rb, KV_LORA), F32)])
    assert len(prompt_scratch) == N_PROMPT_SCRATCH and len(decode_scratch) == N_DECODE_SCRATCH
    grid_spec = pltpu.PrefetchScalarGridSpec(
        num_scalar_prefetch=1,
        grid=(b, nq, nq),
        in_specs=prompt_specs + decode_specs,
        out_specs=[pl.BlockSpec((1, tq, 2 * HW), lambda bi, qi, ki, pt: (bi, qi, 0)),
                   pl.BlockSpec((1, t, 2 * HW),
                                lambda bi, qi, ki, pt: (step_of(bi, qi, ki) * sub // n_groups, 0, 0))],
        scratch_shapes=prompt_scratch + decode_scratch)
    return pl.pallas_call(
        functools.partial(_attend_kernel, tq=tq, sk=sk, cw=cw, pp=pp, t=t, sub=sub,
                          n_groups=n_groups, total=total),
        grid_spec=grid_spec,
        out_shape=[jax.ShapeDtypeStruct((b, s, 2 * HW), BF16),
                   jax.ShapeDtypeStruct((db, t, 2 * HW), F32)],
        name="attend",
        compiler_params=pltpu.CompilerParams(
            dimension_semantics=("arbitrary", "arbitrary", "arbitrary"),
            vmem_limit_bytes=VMEM_LIMIT),
    )(page_table, *prompt_args, *decode_args)


def _mix_ffn_kernel(x_ref, hd_ref, wout_ref, wup_ref, wdn_ref, gpm_ref, gpf_ref, gqf_ref, y_ref):
    mix = _dot(hd_ref[...].astype(BF16), wout_ref[...])
    x1 = x_ref[...] + _rms_f32(mix, gpm_ref[...])
    h = _rms_f32(x1, gpf_ref[...]).astype(BF16)
    u = jnp.maximum(_dot(h, wup_ref[...]), 0.0)
    f = _dot((u * u).astype(BF16), wdn_ref[...])
    y_ref[...] = x1 + _rms_f32(f, gqf_ref[...])


def _mix_ffn(x, heads, w_out, w_up, w_down, g_post_mix, g_pre_ffn, g_post_ffn, tm):
    n = x.shape[0]
    tm = min(tm, n)
    assert n % tm == 0
    row = lambda w: pl.BlockSpec((tm, w), lambda i: (i, 0))
    const = lambda a: pl.BlockSpec(a.shape, lambda i: (0,) * a.ndim, pipeline_mode=pl.Buffered(1))
    consts = [w_out, w_up, w_down, g_post_mix, g_pre_ffn, g_post_ffn]
    return pl.pallas_call(
        _mix_ffn_kernel,
        grid=(n // tm,),
        in_specs=[row(D_MODEL), row(2 * HW)] + [const(a) for a in consts],
        out_specs=row(D_MODEL),
        out_shape=jax.ShapeDtypeStruct((n, D_MODEL), F32),
        name="mix_ffn",
        compiler_params=pltpu.CompilerParams(
            dimension_semantics=("arbitrary",), vmem_limit_bytes=VMEM_LIMIT),
    )(x, heads, *consts)


def _prep_weights(w_in, w_q_b, w_kv_b):
    w_in_p = jnp.pad(w_in, ((0, 0), (0, IN_COLS_PAD - IN_COLS))).astype(BF16)
    w_va_t = w_in[:, C_VA:C_VA + HW].T.astype(BF16)
    w_qb_p = jnp.pad(w_q_b, ((0, 0), (0, 0), (0, LANE - NOPE_B - ROPE_B))).reshape(Q_LORA, HW)
    w_uk = w_kv_b[..., :NOPE_B]
    w_uv = w_kv_b[..., NOPE_B:]
    w_uk_p = jnp.pad(w_uk, ((0, 0), (0, 0), (0, LANE - NOPE_B))).reshape(KV_LORA, HW)
    w_uv_f = w_uv.reshape(KV_LORA, HW)
    eye_r = jnp.eye(ROPE_B, dtype=w_in.dtype)
    per_head = []
    for h in range(H):
        lat = jnp.concatenate([w_uk[:, h, :].T, jnp.zeros((LANE - NOPE_B, KV_LORA), w_in.dtype)], 0)
        rope = jnp.concatenate([jnp.zeros((NOPE_B, ROPE_B), w_in.dtype), eye_r,
                                jnp.zeros((LANE - NOPE_B - ROPE_B, ROPE_B), w_in.dtype)], 0)
        per_head.append(jnp.concatenate([lat, rope], axis=1))
    w_abs = jnp.concatenate(per_head, axis=0)
    return (w_in_p, w_qb_p.astype(BF16), w_uk_p.astype(BF16), w_va_t, w_uv_f.T.astype(BF16),
            w_abs.astype(BF16), w_uv_f.astype(BF16))


def kernel(x_prompt, x_sample, cache_diff_k, cache_diff_v, cache_mla_latent, cache_mla_krope,
           page_table, meta_tokens, g_pre_mix, g_post_mix, g_pre_ffn, g_post_ffn, w_in, g_q_a,
           w_q_b, g_kv_a, w_kv_b, lambda_q1, lambda_k1, lambda_q2, lambda_k2, g_sub, w_out,
           w_up, w_down):
    b, s, d = x_prompt.shape
    db, t, _ = x_sample.shape
    n_pool = cache_diff_k.shape[1]
    past = page_table.shape[1] * PAGE
    assert w_in.shape[0] == 1, "single-layer trunk"
    layer = 0

    w_in_p, w_qb_p, w_uk_p, w_va_t, w_uv_t, w_abs, w_uv_f = _prep_weights(
        w_in[layer], w_q_b[layer], w_kv_b[layer])
    proj_w = (g_pre_mix, w_in_p, g_q_a, w_qb_p, g_kv_a, w_uk_p, w_va_t, w_uv_t)
    lams = (lambda_q1, lambda_k1, lambda_q2, lambda_k2)

    tm = 512
    px = _project(x_prompt.reshape(b * s, d), N_META + jnp.arange(s), tm, b, proj_w)
    pm = _project(meta_tokens.astype(x_prompt.dtype), jnp.arange(N_META), N_META, 1, proj_w)
    tms = min(tm, db * t)
    ps = _project(x_sample.reshape(db * t, d), past + (jnp.arange(tms) % t), tms, 1, proj_w)

    caches = (jnp.transpose(cache_diff_k[layer], (0, 2, 3, 4, 1)).reshape(n_pool, HW, PAGE),
              cache_diff_v[layer].reshape(n_pool, PAGE * H, DV),
              cache_mla_latent[layer],
              jnp.transpose(cache_mla_krope[layer], (0, 2, 1)))
    heads_p, heads_s = _attend(px, pm, ps, caches, page_table, w_abs, w_uv_f, g_sub, lams,
                               b, s, db, t, tq=512, sk=512, cw=512, pp=8)

    ffn_w = (w_out[layer].astype(BF16), w_up[layer].astype(BF16), w_down[layer].astype(BF16),
             g_post_mix, g_pre_ffn, g_post_ffn)
    y_prompt = _mix_ffn(x_prompt.reshape(b * s, d), heads_p.reshape(b * s, 2 * HW), *ffn_w, tm=512)
    y_sample = _mix_ffn(x_sample.reshape(db * t, d), heads_s.reshape(db * t, 2 * HW), *ffn_w, tm=512)

    def with_meta(x_part, m_part, tail):
        m = jnp.broadcast_to(m_part[None], (b, N_META, m_part.shape[-1]))
        full = jnp.concatenate([m, x_part.reshape(b, s, -1)], axis=1)
        return full.reshape((1, b, s + N_META) + tail)

    k_shape, v_shape = (H, 2, DH_A), (H, DV)
    return (y_prompt.reshape(b, s, d), y_sample.reshape(db, t, d),
            with_meta(px[1], pm[1], k_shape), with_meta(px[3], pm[3], v_shape),
            with_meta(px[4], pm[4], (KV_LORA,)), with_meta(px[5], pm[5], (ROPE_B,)),
            ps[1].reshape((1, db, t) + k_shape), ps[3].reshape((1, db, t) + v_shape),
            ps[4].reshape(1, db, t, KV_LORA), ps[5].reshape(1, db, t, ROPE_B))
```

```python
import functools
import math

import jax
import jax.numpy as jnp
import numpy as np
from jax import lax
from jax.experimental import pallas as pl
from jax.experimental.pallas import tpu as pltpu

D_MODEL = 1024
N_META = 16
H = 4
DH_A = 64
DV = 128
ROT_A = 16
Q_LORA = 384
KV_LORA = 256
NOPE_B = 64
ROPE_B = 32
D_FF = 4096
PAGE = 128
ROPE_THETA = 500000.0
RMS_EPS = 1e-6
NEG_INF = -1e30
SCALE_A = DH_A ** -0.5
SCALE_B = (NOPE_B + ROPE_B) ** -0.5
LAM_INIT = 0.8 - 0.6 * math.exp(-0.3 * 0)

LANE = 128
CHAIN_PAGES = 4
ONES_ROWS = 16
HW = H * LANE
IN_COLS = 2208
IN_COLS_PAD = 2304
C_QA, C_KA, C_VA, C_QL, C_CKV, C_KR = 0, 512, 1024, 1536, 1920, 2176

VMEM_LIMIT = 48 * 1024 * 1024

F32 = jnp.float32
BF16 = jnp.bfloat16


def _rms_f32(x, g):
    return x * lax.rsqrt(jnp.mean(x * x, axis=-1, keepdims=True) + RMS_EPS) * g


def _rot(x, c, sm, sp, half):
    w = x.shape[-1]
    return x * c + pltpu.roll(x, w - half, 1) * sm + pltpu.roll(x, half, 1) * sp


def _dot(a, b):
    return jnp.dot(a, b, preferred_element_type=F32)


def _dot_nt(a, b):
    return lax.dot_general(a, b, (((1,), (1,)), ((), ())), preferred_element_type=F32)


def _project_kernel(x_ref, g_ref, win_ref, gq_ref, wqb_ref, gkv_ref, wuk_ref, wvat_ref, wuvt_ref,
                    ca_ref, sma_ref, spa_ref, cb_ref, smb_ref, spb_ref, *rest, meta_rows, tiles_per_b):
    if meta_rows:
        mv_ref, mc_ref = rest[:2]
        qa_o, kaf_o, kab_o, vaf_o, ckv_o, kr_o, qm_o, km_o, vat_o, vbt_o, vbuf, cbuf, sem, msem = rest[2:]
    else:
        qa_o, kaf_o, kab_o, vaf_o, ckv_o, kr_o, qm_o, km_o, vat_o, vbt_o = rest
    h = _rms_f32(x_ref[...], g_ref[...]).astype(BF16)
    z = _dot(h, win_ref[...])
    ca, sma, spa = ca_ref[...], sma_ref[...], spa_ref[...]
    for j in range(H):
        sl = slice(j * LANE, (j + 1) * LANE)
        q = _rot(z[:, C_QA + j * LANE:C_QA + (j + 1) * LANE], ca, sma, spa, ROT_A // 2)
        k = _rot(z[:, C_KA + j * LANE:C_KA + (j + 1) * LANE], ca, sma, spa, ROT_A // 2)
        qa_o[:, sl] = (q * SCALE_A).astype(BF16)
        kaf_o[:, sl] = k
        kab_o[:, sl] = k.astype(BF16)
    if not meta_rows:
        vaf_o[...] = z[:, C_VA:C_VA + HW]
    vat_o[0] = _dot_nt(wvat_ref[...], h).astype(BF16)

    cb, smb, spb = cb_ref[...], smb_ref[...], spb_ref[...]
    kr = _rot(z[:, C_KR:C_KR + LANE], cb, smb, spb, ROPE_B // 2)
    kr_o[...] = kr[:, :ROPE_B]
    kr_at64 = pltpu.roll(kr, NOPE_B, 1)

    ckv = _rms_f32(z[:, C_CKV:C_CKV + KV_LORA], gkv_ref[...])
    if meta_rows:
        i = pl.program_id(0)
        tm = x_ref.shape[0]
        bi, ti, slot = i // tiles_per_b, i % tiles_per_b, i % 2

        def row_copies(to_slot, b_idx, t_idx):
            return (pltpu.make_async_copy(
                        vbuf.at[to_slot], vaf_o.at[b_idx, pl.ds((meta_rows + t_idx * tm) * H, tm * H), :],
                        sem.at[to_slot, 0]),
                    pltpu.make_async_copy(
                        cbuf.at[to_slot], ckv_o.at[b_idx, pl.ds(meta_rows + t_idx * tm, tm), :],
                        sem.at[to_slot, 1]))

        def meta_copies(b_idx):
            return (pltpu.make_async_copy(mv_ref, vaf_o.at[b_idx, pl.ds(0, meta_rows * H), :], msem.at[0]),
                    pltpu.make_async_copy(mc_ref, ckv_o.at[b_idx, pl.ds(0, meta_rows), :], msem.at[1]))

        @pl.when(i >= 2)
        def _():
            for cp in row_copies(slot, 0, 0):
                cp.wait()

        for j in range(H):
            vbuf[slot, pl.ds(j, tm, stride=H), :] = z[:, C_VA + j * LANE:C_VA + (j + 1) * LANE]
        cbuf[slot] = ckv
        for cp in row_copies(slot, bi, ti):
            cp.start()

        @pl.when(ti == 0)
        def _():
            for cp in meta_copies(bi):
                cp.start()

        @pl.when(ti == 1)
        def _():
            for cp in meta_copies(bi):
                cp.wait()

        @pl.when(i == pl.num_programs(0) - 1)
        def _():
            for cp in row_copies(1 - slot, 0, 0) + row_copies(slot, 0, 0):
                cp.wait()
    else:
        ckv_o[...] = ckv
    ckv_b = ckv.astype(BF16)
    kn = _dot(ckv_b, wuk_ref[...])
    vbt_o[0] = _dot_nt(wuvt_ref[...], ckv_b).astype(BF16)

    ql = _rms_f32(z[:, C_QL:C_QL + Q_LORA], gq_ref[...]).astype(BF16)
    qb = _dot(ql, wqb_ref[...])
    cq = pltpu.roll(cb, NOPE_B, 1)
    smq = pltpu.roll(smb, NOPE_B, 1)
    spq = pltpu.roll(spb, NOPE_B, 1)
    for j in range(H):
        sl = slice(j * LANE, (j + 1) * LANE)
        qm_o[:, sl] = _rot(qb[:, sl], cq, smq, spq, ROPE_B // 2).astype(BF16)
        km_o[:, sl] = (kn[:, sl] + kr_at64).astype(BF16)


def _rope_tables(pos):
    pos = pos.astype(F32)[:, None]
    lane = np.arange(LANE)

    def build(period, half, active):
        idx = lane % period
        inv = 1.0 / (ROPE_THETA ** (jnp.arange(half, dtype=F32) / half))
        ang = pos * inv[None, :]
        cos = jnp.cos(ang)[:, idx % half]
        sin = jnp.sin(ang)[:, idx % half]
        first = jnp.asarray(active & (idx < half))
        second = jnp.asarray(active & (idx >= half) & (idx < 2 * half))
        c = jnp.where(first | second, cos, 1.0)
        sm = jnp.where(first, -sin, 0.0)
        sp = jnp.where(second, sin, 0.0)
        return c, sm, sp

    a = build(DH_A, ROT_A // 2, np.ones(LANE, bool))
    b = build(LANE, ROPE_B // 2, lane < ROPE_B)
    return a + b


def _project(x, pos, tm, nb, wts, meta=None):
    n = x.shape[0]
    n_pos_tiles = pos.shape[0] // tm
    tiles_per_b = n // nb // tm
    tables = _rope_tables(pos)
    grid = (n // tm,)
    row = lambda w: pl.BlockSpec((tm, w), lambda i: (i, 0))
    full = lambda a: pl.BlockSpec(a.shape, lambda i: (0,) * a.ndim)
    tab = pl.BlockSpec((tm, LANE), lambda i: (i % n_pos_tiles, 0))
    colmajor = pl.BlockSpec((1, HW, tm), lambda i: (i // tiles_per_b, 0, i % tiles_per_b))
    outs = [(HW, BF16), (HW, F32), (HW, BF16), (HW, F32), (KV_LORA, F32), (ROPE_B, F32),
            (HW, BF16), (HW, BF16)]
    out_specs = [row(w) for w, _ in outs] + [colmajor, colmajor]
    t_shape = jax.ShapeDtypeStruct((nb, HW, n // nb), BF16)
    out_shape = [jax.ShapeDtypeStruct((n, w), dt) for w, dt in outs] + [t_shape, t_shape]
    extra_in, extra_specs, scratch, meta_rows = [], [], [], 0
    if meta is not None:
        assert tiles_per_b >= 2
        meta_rows = meta[1].shape[0]
        rows = meta_rows + n // nb
        extra_in, extra_specs = list(meta), [full(a) for a in meta]
        out_specs[3] = out_specs[4] = pl.BlockSpec(memory_space=pl.ANY)
        out_shape[3] = jax.ShapeDtypeStruct((nb, rows * H, DV), F32)
        out_shape[4] = jax.ShapeDtypeStruct((nb, rows, KV_LORA), F32)
        scratch = [pltpu.VMEM((2, tm * H, DV), F32), pltpu.VMEM((2, tm, KV_LORA), F32),
                   pltpu.SemaphoreType.DMA((2, 2)), pltpu.SemaphoreType.DMA((2,))]
    return pl.pallas_call(
        functools.partial(_project_kernel, meta_rows=meta_rows, tiles_per_b=tiles_per_b),
        grid=grid,
        in_specs=[row(D_MODEL)] + [full(a) for a in wts] + [tab] * 6 + extra_specs,
        out_specs=out_specs,
        out_shape=out_shape,
        scratch_shapes=scratch,
        name="project",
        compiler_params=pltpu.CompilerParams(
            dimension_semantics=("arbitrary",), vmem_limit_bytes=VMEM_LIMIT),
    )(x, *wts, *tables, *extra_in)


def _lambda(lq1, lk1, lq2, lk2):
    return (jnp.exp(jnp.sum(lq1[...] * lk1[...], keepdims=True))
            - jnp.exp(jnp.sum(lq2[...] * lk2[...], keepdims=True)) + LAM_INIT)


def _prompt_step(qi, ki, qa_ref, qm_ref, ka_ref, km_ref, vat_ref, vbt_ref,
                 mka_ref, mkm_ref, mvat_ref, mvbt_ref, gcol_ref, lq1, lk1, lq2, lk2,
                 out_ref, qw, ma, acca, mb, accb, *, tq, sk, cw):
    def update(ka, km, vat, vbt, mask):
        streams = [(h, grp, cb) for h in range(H)
                   for grp, ncb in (("a", 2 * tq // cw), ("b", tq // cw)) for cb in range(ncb)]

        def scores(st):
            h, grp, cb = st
            sl = slice(h * LANE, (h + 1) * LANE)
            cols = slice(cb * cw, (cb + 1) * cw)
            if grp == "a":
                return _dot_nt(ka[:, sl], qw[h, cols, :])
            return _dot_nt(km[:, sl], qm_ref[0, cols, sl]) * SCALE_B

        s_next = scores(streams[0])
        for i, (h, grp, cb) in enumerate(streams):
            s = s_next
            if i + 1 < len(streams):
                s_next = scores(streams[i + 1])
            sl = slice(h * LANE, (h + 1) * LANE)
            idx = (h, slice(None), slice(cb * cw, (cb + 1) * cw))
            qcol = (cb * cw) % tq
            mk = None if mask is None else mask[:, qcol:qcol + cw]
            m_ref, acc_ref, vt = (ma, acca, vat) if grp == "a" else (mb, accb, vbt)
            if mk is not None:
                s = jnp.where(mk, s, NEG_INF)
            m_old = m_ref[idx]
            m_new = jnp.maximum(m_old, jnp.max(s, axis=0, keepdims=True))
            p = jnp.exp(s - m_new)
            if mk is not None:
                p = jnp.where(mk, p, 0.0)
            m_ref[idx] = m_new
            v_ext = jnp.concatenate([vt[sl, :], jnp.ones((ONES_ROWS, vt.shape[1]), BF16)], axis=0)
            acc_ref[idx] = acc_ref[idx] * jnp.exp(m_old - m_new) + _dot(v_ext, p.astype(BF16))

    @pl.when(ki == 0)
    def _():
        lane = lax.broadcasted_iota(jnp.int32, (tq, LANE), 1)
        for h in range(H):
            q = qa_ref[0, :, h * LANE:(h + 1) * LANE]
            qw[h, :tq] = jnp.where(lane < DH_A, q, 0)
            qw[h, tq:] = jnp.where(lane >= DH_A, q, 0)
        ma[...] = jnp.full(ma.shape, NEG_INF, F32)
        mb[...] = jnp.full(mb.shape, NEG_INF, F32)
        acca[...] = jnp.zeros(acca.shape, F32)
        accb[...] = jnp.zeros(accb.shape, F32)
        update(mka_ref[...], mkm_ref[...], mvat_ref[0], mvbt_ref[0], None)

    @pl.when(ki < qi)
    def _():
        for j in range(tq // sk):
            rows = slice(j * sk, (j + 1) * sk)
            update(ka_ref[0, rows, :], km_ref[0, rows, :], vat_ref[0, :, rows], vbt_ref[0, :, rows],
                   None)

    @pl.when(ki == qi)
    def _():
        r = lax.broadcasted_iota(jnp.int32, (sk, tq), 0)
        c = lax.broadcasted_iota(jnp.int32, (sk, tq), 1)
        for j in range(tq // sk):
            rows = slice(j * sk, (j + 1) * sk)
            update(ka_ref[0, rows, :], km_ref[0, rows, :], vat_ref[0, :, rows], vbt_ref[0, :, rows],
                   r + j * sk <= c)
        lam = _lambda(lq1, lk1, lq2, lk2)
        for h in range(H):
            o12 = acca[h, :DV] / acca[h, DV:DV + 1]
            d = o12[:, :tq] - lam * o12[:, tq:]
            ms = jnp.mean(d * d, axis=0, keepdims=True)
            o = d * lax.rsqrt(ms + RMS_EPS) * gcol_ref[...] * (1.0 - LAM_INIT)
            out_ref[0, :, h * LANE:(h + 1) * LANE] = o.T.astype(out_ref.dtype)
            ob = accb[h, :DV] / accb[h, DV:DV + 1]
            out_ref[0, :, HW + h * LANE:HW + (h + 1) * LANE] = ob.T.astype(out_ref.dtype)


def _decode_steps(first, pt_ref, qa_ref, qm_ref, kan_ref, van_ref, cn_ref, krn_ref,
                  wabs_ref, wuv_ref, gsub_ref, lq1, lk1, lq2, lk2,
                  kt_hbm, v_hbm, c_hbm, krt_hbm,
                  out_ref, kt_buf, v_buf, c_buf, krt_buf, sem,
                  qbd, qext, ma, la, acca, mb, lb, accb, *, pp, t, sub, n_groups, total):
    last = first + sub - 1
    streams = ((kt_hbm, kt_buf), (v_hbm, v_buf), (c_hbm, c_buf), (krt_hbm, krt_buf))

    def page_copy(kind, page, to_slot, p):
        hbm, buf = streams[kind]
        return pltpu.make_async_copy(hbm.at[page], buf.at[to_slot, p], sem.at[to_slot, kind])

    def fetch(target, to_slot):
        seq, grp = target // n_groups, target % n_groups
        for p in range(pp):
            page = pt_ref[seq, grp * pp + p]
            for kind in range(len(streams)):
                page_copy(kind, page, to_slot, p).start()

    def drain(of_slot):
        for p in range(pp):
            for kind in range(len(streams)):
                page_copy(kind, 0, of_slot, p).wait()

    @pl.when(first == 0)
    def _():
        for slot in range(sub):
            fetch(slot, slot)

    ra, rb = 2 * H * t, H * t
    head_a = (lax.broadcasted_iota(jnp.int32, (ra, LANE), 0) // t) % H

    def rep_rows(q, rows):
        tok = lax.broadcasted_iota(jnp.int32, (rows, HW), 0) % t
        out = jnp.broadcast_to(q[t - 1:t], (rows, HW))
        for i in range(t - 2, -1, -1):
            out = jnp.where(tok == i, jnp.broadcast_to(q[i:i + 1], (rows, HW)), out)
        return out

    @pl.when(first % n_groups == 0)
    def _():
        r = lax.broadcasted_iota(jnp.int32, (ra, HW), 0)
        ln = lax.broadcasted_iota(jnp.int32, (ra, HW), 1)
        chunk = ((r // t) % H) * 2 + r // (H * t)
        qbd[...] = jnp.where(ln // DH_A == chunk, rep_rows(qa_ref[0], ra), 0.0).astype(BF16)
        r = lax.broadcasted_iota(jnp.int32, (rb, HW), 0)
        ln = lax.broadcasted_iota(jnp.int32, (rb, HW), 1)
        q = jnp.where(ln // LANE == r // t, rep_rows(qm_ref[0], rb), 0.0).astype(BF16)
        qext[...] = _dot(q, wabs_ref[...]).astype(BF16)
        ma[...] = jnp.full(ma.shape, NEG_INF, F32)
        mb[...] = jnp.full(mb.shape, NEG_INF, F32)
        la[...] = jnp.zeros(la.shape, F32)
        lb[...] = jnp.zeros(lb.shape, F32)
        acca[...] = jnp.zeros(acca.shape, F32)
        accb[...] = jnp.zeros(accb.shape, F32)

    def pv_a(p, v_of_head):
        p = p.astype(BF16)
        new = jnp.zeros((ra, DV), F32)
        for h in range(0, H, 2):
            pv = _dot(p, jnp.concatenate([v_of_head(h), v_of_head(h + 1)], axis=1))
            new = jnp.where(head_a == h, pv[:, :DV], jnp.where(head_a == h + 1, pv[:, DV:], new))
        return new

    def advance(state, s, mask, pv):
        m_old, l_old, acc = state
        if mask is not None:
            s = jnp.where(mask, s, NEG_INF)
        m_new = jnp.maximum(m_old, jnp.max(s, axis=1, keepdims=True))
        p = jnp.exp(s - m_new)
        if mask is not None:
            p = jnp.where(mask, p, 0.0)
        corr = jnp.exp(m_old - m_new)
        return m_new, l_old * corr + jnp.sum(p, axis=1, keepdims=True), acc * corr + pv(p)

    qe = qext[...]
    st_a = (ma[0], la[0], acca[0])
    st_b = (mb[0], lb[0], accb[0])
    for slot in range(sub):
        drain(slot)
    chains = []
    for slot in range(sub):
        for first_page in range(0, pp, CHAIN_PAGES):
            pages = range(first_page, first_page + CHAIN_PAGES)
            kt = jnp.concatenate([kt_buf[slot, p].astype(BF16) for p in pages], axis=1)
            krt = jnp.concatenate([krt_buf[slot, p].astype(BF16) for p in pages], axis=1)
            c = jnp.concatenate([c_buf[slot, p].astype(BF16) for p in pages], axis=0)
            s_a = _dot(qbd[...], kt)
            s_b = (_dot_nt(qe[:, :KV_LORA], c) + _dot(qe[:, KV_LORA:], krt)) * SCALE_B
            chains.append((slot, pages, s_a, s_b, c))
    for slot, pages, s_a, s_b, c in chains:
        def page_values(h, slot=slot, pages=pages):
            return jnp.concatenate([v_buf[slot, p, pl.ds(h, PAGE, stride=H), :].astype(BF16)
                                    for p in pages], axis=0)

        st_a = advance(st_a, s_a, None, lambda p: pv_a(p, page_values))
        st_b = advance(st_b, s_b, None, lambda p, c=c: _dot(p.astype(BF16), c))
        if pages[-1] == pp - 1:
            fetch((first + sub + slot) % total, slot)
    ma[0], la[0], acca[0] = st_a
    mb[0], lb[0], accb[0] = st_b

    @pl.when(last % n_groups == n_groups - 1)
    def _():
        nk = 16

        def pad_rows(x):
            r = lax.broadcasted_iota(jnp.int32, (nk, x.shape[1]), 0)
            out = jnp.zeros((nk, x.shape[1]), x.dtype)
            for i in range(t):
                out = jnp.where(r == i, jnp.broadcast_to(x[i:i + 1], out.shape), out)
            return out.astype(BF16)

        kn, vn, cn, krn = (pad_rows(r[0]) for r in (kan_ref, van_ref, cn_ref, krn_ref))
        causal_a = (lax.broadcasted_iota(jnp.int32, (ra, nk), 1)
                    <= lax.broadcasted_iota(jnp.int32, (ra, nk), 0) % t)
        causal_b = (lax.broadcasted_iota(jnp.int32, (rb, nk), 1)
                    <= lax.broadcasted_iota(jnp.int32, (rb, nk), 0) % t)
        s_n = (_dot_nt(qe[:, :KV_LORA], cn) + _dot_nt(qe[:, KV_LORA:], krn)) * SCALE_B
        _, l_a, acc_a = advance(st_a, _dot_nt(qbd[...], kn), causal_a,
                                lambda p: pv_a(p, lambda h: vn[:, h * LANE:(h + 1) * LANE]))
        _, l_b, acc_b = advance(st_b, s_n, causal_b, lambda p: _dot(p.astype(BF16), cn))

        lam = _lambda(lq1, lk1, lq2, lk2)
        oa = acc_a / l_a
        o_a = _rms_f32(oa[:rb] - lam * oa[rb:], gsub_ref[...]) * (1.0 - LAM_INIT)
        o_b = _dot((acc_b / l_b).astype(BF16), wuv_ref[...])
        for h in range(H):
            rows = slice(h * t, (h + 1) * t)
            out_ref[0, :, h * LANE:(h + 1) * LANE] = o_a[rows, :]
            out_ref[0, :, HW + h * LANE:HW + (h + 1) * LANE] = o_b[rows, h * LANE:(h + 1) * LANE]

    @pl.when(last == total - 1)
    def _():
        for slot in range(sub):
            drain(slot)


N_PROMPT_IN, N_PROMPT_SCRATCH = 15, 5
N_DECODE_IN, N_DECODE_SCRATCH = 17, 13


def _attend_kernel(pt_ref, *refs, tq, sk, cw, pp, t, sub, n_groups, total):
    prompt_in, refs = refs[:N_PROMPT_IN], refs[N_PROMPT_IN:]
    decode_in, refs = refs[:N_DECODE_IN], refs[N_DECODE_IN:]
    (out_p, out_s), refs = refs[:2], refs[2:]
    prompt_scr, decode_scr = refs[:N_PROMPT_SCRATCH], refs[N_PROMPT_SCRATCH:]
    assert len(decode_scr) == N_DECODE_SCRATCH
    qi, ki = pl.program_id(1), pl.program_id(2)
    step = (pl.program_id(0) * pl.num_programs(1) + qi) * pl.num_programs(2) + ki
    _decode_steps(step * sub, pt_ref, *decode_in, out_s, *decode_scr,
                  pp=pp, t=t, sub=sub, n_groups=n_groups, total=total)
    _prompt_step(qi, ki, *prompt_in, out_p, *prompt_scr, tq=tq, sk=sk, cw=cw)


def _attend(px, pm, ps, caches, page_table, wabs, wuv, g_sub, lams, b, s, db, t, tq, sk, cw, pp):
    qa, _, kab, _, _, _, qm, km, vat, vbt = px
    _, _, mkab, _, _, _, _, mkm, mvat, mvbt = pm
    sqa, skaf, _, svaf, sckv, skr, sqm, _, _, _ = ps
    nq = s // tq
    n_grid = b * nq * nq
    n_groups = page_table.shape[1] // pp
    total = db * n_groups
    sub = total // n_grid
    assert sub * n_grid == total and n_groups % sub == 0
    step_of = lambda bi, qi, ki: (bi * nq + qi) * nq + ki

    r3 = lambda a: a.reshape(b, s, HW)
    qspec = pl.BlockSpec((1, tq, HW), lambda bi, qi, ki, pt: (bi, qi, 0))
    kspec = pl.BlockSpec((1, tq, HW), lambda bi, qi, ki, pt: (bi, jnp.minimum(ki, qi), 0))
    vspec = pl.BlockSpec((1, HW, tq), lambda bi, qi, ki, pt: (bi, 0, jnp.minimum(ki, qi)))
    full = lambda a: pl.BlockSpec(a.shape, lambda bi, qi, ki, pt: (0,) * a.ndim)
    seq3 = lambda a: a.reshape(db, t, a.shape[-1])
    tok = lambda w: pl.BlockSpec((1, t, w),
                                 lambda bi, qi, ki, pt: (step_of(bi, qi, ki) * sub // n_groups, 0, 0))
    gcol = jnp.broadcast_to(g_sub.reshape(DV, 1), (DV, tq))
    prompt_args = [r3(qa), r3(qm), r3(kab), r3(km), vat, vbt, mkab, mkm, mvat, mvbt, gcol] + list(lams)
    prompt_specs = [qspec, qspec, kspec, kspec, vspec, vspec] + [full(a) for a in prompt_args[6:]]
    decode_consts = [wabs, wuv, g_sub] + list(lams)
    decode_args = ([seq3(sqa.astype(F32)), seq3(sqm.astype(F32)), seq3(skaf), seq3(svaf), seq3(sckv),
                    seq3(skr)] + decode_consts + list(caches))
    decode_specs = ([tok(HW), tok(HW), tok(HW), tok(HW), tok(KV_LORA), tok(ROPE_B)]
                    + [full(a) for a in decode_consts]
                    + [pl.BlockSpec(memory_space=pl.ANY)] * len(caches))
    assert len(prompt_args) == N_PROMPT_IN and len(decode_args) == N_DECODE_IN
    ra, rb = 2 * t * H, t * H
    prompt_scratch = [pltpu.VMEM((H, 2 * tq, LANE), BF16),
                      pltpu.VMEM((H, 1, 2 * tq), F32), pltpu.VMEM((H, DV + ONES_ROWS, 2 * tq), F32),
                      pltpu.VMEM((H, 1, tq), F32), pltpu.VMEM((H, DV + ONES_ROWS, tq), F32)]
    decode_scratch = ([pltpu.VMEM((sub, pp) + a.shape[1:], a.dtype) for a in caches]
                      + [pltpu.SemaphoreType.DMA((sub, len(caches))),
                         pltpu.VMEM((ra, HW), BF16), pltpu.VMEM((rb, KV_LORA + ROPE_B), BF16),
                         pltpu.VMEM((1, ra, 1), F32), pltpu.VMEM((1, ra, 1), F32),
                         pltpu.VMEM((1, ra, DV), F32),
                         pltpu.VMEM((1, rb, 1), F32), pltpu.VMEM((1, rb, 1), F32),
                         pltpu.VMEM((1, rb, KV_LORA), F32)])
    assert len(prompt_scratch) == N_PROMPT_SCRATCH and len(decode_scratch) == N_DECODE_SCRATCH
    grid_spec = pltpu.PrefetchScalarGridSpec(
        num_scalar_prefetch=1,
        grid=(b, nq, nq),
        in_specs=prompt_specs + decode_specs,
        out_specs=[pl.BlockSpec((1, tq, 2 * HW), lambda bi, qi, ki, pt: (bi, qi, 0)),
                   pl.BlockSpec((1, t, 2 * HW),
                                lambda bi, qi, ki, pt: (step_of(bi, qi, ki) * sub // n_groups, 0, 0))],
        scratch_shapes=prompt_scratch + decode_scratch)
    return pl.pallas_call(
        functools.partial(_attend_kernel, tq=tq, sk=sk, cw=cw, pp=pp, t=t, sub=sub,
                          n_groups=n_groups, total=total),
        grid_spec=grid_spec,
        out_shape=[jax.ShapeDtypeStruct((b, s, 2 * HW), BF16),
                   jax.ShapeDtypeStruct((db, t, 2 * HW), F32)],
        name="attend",
        compiler_params=pltpu.CompilerParams(
            dimension_semantics=("arbitrary", "arbitrary", "arbitrary"),
            vmem_limit_bytes=VMEM_LIMIT),
    )(page_table, *prompt_args, *decode_args)


def _mix_ffn_kernel(x_ref, hd_ref, wout_ref, wup_ref, wdn_ref, gpm_ref, gpf_ref, gqf_ref, y_ref):
    mix = _dot(hd_ref[...].astype(BF16), wout_ref[...])
    x1 = x_ref[...] + _rms_f32(mix, gpm_ref[...])
    h = _rms_f32(x1, gpf_ref[...]).astype(BF16)
    u = jnp.maximum(_dot(h, wup_ref[...]), 0.0)
    f = _dot((u * u).astype(BF16), wdn_ref[...])
    y_ref[...] = x1 + _rms_f32(f, gqf_ref[...])


def _mix_ffn(x, heads, w_out, w_up, w_down, g_post_mix, g_pre_ffn, g_post_ffn, tm):
    n = x.shape[0]
    tm = min(tm, n)
    assert n % tm == 0
    row = lambda w: pl.BlockSpec((tm, w), lambda i: (i, 0))
    const = lambda a: pl.BlockSpec(a.shape, lambda i: (0,) * a.ndim, pipeline_mode=pl.Buffered(1))
    consts = [w_out, w_up, w_down, g_post_mix, g_pre_ffn, g_post_ffn]
    return pl.pallas_call(
        _mix_ffn_kernel,
        grid=(n // tm,),
        in_specs=[row(D_MODEL), row(2 * HW)] + [const(a) for a in consts],
        out_specs=row(D_MODEL),
        out_shape=jax.ShapeDtypeStruct((n, D_MODEL), F32),
        name="mix_ffn",
        compiler_params=pltpu.CompilerParams(
            dimension_semantics=("arbitrary",), vmem_limit_bytes=VMEM_LIMIT),
    )(x, heads, *consts)


def _prep_weights(w_in, w_q_b, w_kv_b):
    w_in_p = jnp.pad(w_in, ((0, 0), (0, IN_COLS_PAD - IN_COLS))).astype(BF16)
    w_va_t = w_in[:, C_VA:C_VA + HW].T.astype(BF16)
    w_qb_p = jnp.pad(w_q_b, ((0, 0), (0, 0), (0, LANE - NOPE_B - ROPE_B))).reshape(Q_LORA, HW)
    w_uk = w_kv_b[..., :NOPE_B]
    w_uv = w_kv_b[..., NOPE_B:]
    w_uk_p = jnp.pad(w_uk, ((0, 0), (0, 0), (0, LANE - NOPE_B))).reshape(KV_LORA, HW)
    w_uv_f = w_uv.reshape(KV_LORA, HW)
    eye_r = jnp.eye(ROPE_B, dtype=w_in.dtype)
    per_head = []
    for h in range(H):
        lat = jnp.concatenate([w_uk[:, h, :].T, jnp.zeros((LANE - NOPE_B, KV_LORA), w_in.dtype)], 0)
        rope = jnp.concatenate([jnp.zeros((NOPE_B, ROPE_B), w_in.dtype), eye_r,
                                jnp.zeros((LANE - NOPE_B - ROPE_B, ROPE_B), w_in.dtype)], 0)
        per_head.append(jnp.concatenate([lat, rope], axis=1))
    w_abs = jnp.concatenate(per_head, axis=0)
    return (w_in_p, w_qb_p.astype(BF16), w_uk_p.astype(BF16), w_va_t, w_uv_f.T.astype(BF16),
            w_abs.astype(BF16), w_uv_f.astype(BF16))


def kernel(x_prompt, x_sample, cache_diff_k, cache_diff_v, cache_mla_latent, cache_mla_krope,
           page_table, meta_tokens, g_pre_mix, g_post_mix, g_pre_ffn, g_post_ffn, w_in, g_q_a,
           w_q_b, g_kv_a, w_kv_b, lambda_q1, lambda_k1, lambda_q2, lambda_k2, g_sub, w_out,
           w_up, w_down):
    b, s, d = x_prompt.shape
    db, t, _ = x_sample.shape
    n_pool = cache_diff_k.shape[1]
    past = page_table.shape[1] * PAGE
    assert w_in.shape[0] == 1, "single-layer trunk"
    layer = 0

    w_in_p, w_qb_p, w_uk_p, w_va_t, w_uv_t, w_abs, w_uv_f = _prep_weights(
        w_in[layer], w_q_b[layer], w_kv_b[layer])
    proj_w = (g_pre_mix, w_in_p, g_q_a, w_qb_p, g_kv_a, w_uk_p, w_va_t, w_uv_t)
    lams = (lambda_q1, lambda_k1, lambda_q2, lambda_k2)

    tm = 512
    pm = _project(meta_tokens.astype(x_prompt.dtype), jnp.arange(N_META), N_META, 1, proj_w)
    meta_v = pm[3].reshape(N_META * H, DV)
    px = _project(x_prompt.reshape(b * s, d), N_META + jnp.arange(s), tm, b, proj_w, meta=(meta_v, pm[4]))
    tms = min(tm, db * t)
    ps = _project(x_sample.reshape(db * t, d), past + (jnp.arange(tms) % t), tms, 1, proj_w)

    caches = (jnp.transpose(cache_diff_k[layer], (0, 2, 3, 4, 1)).reshape(n_pool, HW, PAGE),
              cache_diff_v[layer].reshape(n_pool, PAGE * H, DV),
              cache_mla_latent[layer],
              jnp.transpose(cache_mla_krope[layer], (0, 2, 1)))
    heads_p, heads_s = _attend(px, pm, ps, caches, page_table, w_abs, w_uv_f, g_sub, lams,
                               b, s, db, t, tq=512, sk=512, cw=512, pp=8)

    ffn_w = (w_out[layer].astype(BF16), w_up[layer].astype(BF16), w_down[layer].astype(BF16),
             g_post_mix, g_pre_ffn, g_post_ffn)
    y_prompt = _mix_ffn(x_prompt.reshape(b * s, d), heads_p.reshape(b * s, 2 * HW), *ffn_w, tm=512)
    y_sample = _mix_ffn(x_sample.reshape(db * t, d), heads_s.reshape(db * t, 2 * HW), *ffn_w, tm=512)

    def with_meta(x_part, m_part, tail):
        m = jnp.broadcast_to(m_part[None], (b, N_META, m_part.shape[-1]))
        full = jnp.concatenate([m, x_part.reshape(b, s, -1)], axis=1)
        return full.reshape((1, b, s + N_META) + tail)

    k_shape, v_shape = (H, 2, DH_A), (H, DV)
    return (y_prompt.reshape(b, s, d), y_sample.reshape(db, t, d),
            with_meta(px[1], pm[1], k_shape), px[3].reshape((1, b, s + N_META) + v_shape),
            px[4][None], with_meta(px[5], pm[5], (ROPE_B,)),
            ps[1].reshape((1, db, t) + k_shape), ps[3].reshape((1, db, t) + v_shape),
            ps[4].reshape(1, db, t, KV_LORA), ps[5].reshape(1, db, t, ROPE_B))
```

```python
import functools
import math

import jax
import jax.numpy as jnp
import numpy as np
from jax import lax
from jax.experimental import pallas as pl
from jax.experimental.pallas import tpu as pltpu

D_MODEL = 1024
N_META = 16
H = 4
DH_A = 64
DV = 128
ROT_A = 16
Q_LORA = 384
KV_LORA = 256
NOPE_B = 64
ROPE_B = 32
D_FF = 4096
PAGE = 128
ROPE_THETA = 500000.0
RMS_EPS = 1e-6
NEG_INF = -1e30
SCALE_A = DH_A ** -0.5
SCALE_B = (NOPE_B + ROPE_B) ** -0.5
LAM_INIT = 0.8 - 0.6 * math.exp(-0.3 * 0)

LANE = 128
AHEAD = 2
CHAIN_PAGES = 4
ONES_ROWS = 16
HW = H * LANE
IN_COLS = 2208
IN_COLS_PAD = 2304
C_QA, C_KA, C_VA, C_QL, C_CKV, C_KR = 0, 512, 1024, 1536, 1920, 2176

VMEM_LIMIT = 48 * 1024 * 1024

F32 = jnp.float32
BF16 = jnp.bfloat16


def _rms_f32(x, g):
    return x * lax.rsqrt(jnp.mean(x * x, axis=-1, keepdims=True) + RMS_EPS) * g


def _rot(x, c, sm, sp, half):
    w = x.shape[-1]
    return x * c + pltpu.roll(x, w - half, 1) * sm + pltpu.roll(x, half, 1) * sp


def _dot(a, b):
    return jnp.dot(a, b, preferred_element_type=F32)


def _dot_nt(a, b):
    return lax.dot_general(a, b, (((1,), (1,)), ((), ())), preferred_element_type=F32)


def _project_kernel(x_ref, g_ref, win_ref, gq_ref, wqb_ref, gkv_ref, wuk_ref, wvat_ref, wuvt_ref,
                    ca_ref, sma_ref, spa_ref, cb_ref, smb_ref, spb_ref, *rest, meta_rows, tiles_per_b):
    if meta_rows:
        mv_ref, mc_ref = rest[:2]
        qa_o, kaf_o, kab_o, vaf_o, ckv_o, kr_o, qm_o, km_o, vat_o, vbt_o, vbuf, cbuf, sem, msem = rest[2:]
    else:
        qa_o, kaf_o, kab_o, vaf_o, ckv_o, kr_o, qm_o, km_o, vat_o, vbt_o = rest
    h = _rms_f32(x_ref[...], g_ref[...]).astype(BF16)
    zl = _dot(h, win_ref[:, C_QL:])
    z = _dot(h, win_ref[:, :C_QL])

    ckv = _rms_f32(zl[:, C_CKV - C_QL:C_KR - C_QL], gkv_ref[...])
    ckv_b = ckv.astype(BF16)
    ql = _rms_f32(zl[:, :Q_LORA], gq_ref[...]).astype(BF16)
    kn = _dot(ckv_b, wuk_ref[...])
    qb = _dot(ql, wqb_ref[...])
    vat_o[0] = _dot_nt(wvat_ref[...], h).astype(BF16)
    vbt_o[0] = _dot_nt(wuvt_ref[...], ckv_b).astype(BF16)

    ca, sma, spa = ca_ref[...], sma_ref[...], spa_ref[...]
    for j in range(H):
        sl = slice(j * LANE, (j + 1) * LANE)
        q = _rot(z[:, C_QA + j * LANE:C_QA + (j + 1) * LANE], ca, sma, spa, ROT_A // 2)
        k = _rot(z[:, C_KA + j * LANE:C_KA + (j + 1) * LANE], ca, sma, spa, ROT_A // 2)
        qa_o[:, sl] = (q * SCALE_A).astype(BF16)
        kaf_o[:, sl] = k
        kab_o[:, sl] = k.astype(BF16)

    cb, smb, spb = cb_ref[...], smb_ref[...], spb_ref[...]
    kr = _rot(zl[:, C_KR - C_QL:], cb, smb, spb, ROPE_B // 2)
    kr_o[...] = kr[:, :ROPE_B]
    kr_at64 = pltpu.roll(kr, NOPE_B, 1)

    cq = pltpu.roll(cb, NOPE_B, 1)
    smq = pltpu.roll(smb, NOPE_B, 1)
    spq = pltpu.roll(spb, NOPE_B, 1)
    for j in range(H):
        sl = slice(j * LANE, (j + 1) * LANE)
        qm_o[:, sl] = _rot(qb[:, sl], cq, smq, spq, ROPE_B // 2).astype(BF16)
        km_o[:, sl] = (kn[:, sl] + kr_at64).astype(BF16)

    if meta_rows:
        i = pl.program_id(0)
        tm = x_ref.shape[0]
        bi, ti, slot = i // tiles_per_b, i % tiles_per_b, i % 2

        def row_copies(to_slot, b_idx, t_idx):
            return (pltpu.make_async_copy(
                        vbuf.at[to_slot], vaf_o.at[b_idx, pl.ds((meta_rows + t_idx * tm) * H, tm * H), :],
                        sem.at[to_slot, 0]),
                    pltpu.make_async_copy(
                        cbuf.at[to_slot], ckv_o.at[b_idx, pl.ds(meta_rows + t_idx * tm, tm), :],
                        sem.at[to_slot, 1]))

        def meta_copies(b_idx):
            return (pltpu.make_async_copy(mv_ref, vaf_o.at[b_idx, pl.ds(0, meta_rows * H), :], msem.at[0]),
                    pltpu.make_async_copy(mc_ref, ckv_o.at[b_idx, pl.ds(0, meta_rows), :], msem.at[1]))

        @pl.when(i >= 2)
        def _():
            for cp in row_copies(slot, 0, 0):
                cp.wait()

        for j in range(H):
            vbuf[slot, pl.ds(j, tm, stride=H), :] = z[:, C_VA + j * LANE:C_VA + (j + 1) * LANE]
        cbuf[slot] = ckv
        for cp in row_copies(slot, bi, ti):
            cp.start()

        @pl.when(ti == 0)
        def _():
            for cp in meta_copies(bi):
                cp.start()

        @pl.when(ti == 1)
        def _():
            for cp in meta_copies(bi):
                cp.wait()

        @pl.when(i == pl.num_programs(0) - 1)
        def _():
            for cp in row_copies(1 - slot, 0, 0) + row_copies(slot, 0, 0):
                cp.wait()
    else:
        vaf_o[...] = z[:, C_VA:C_VA + HW]
        ckv_o[...] = ckv


def _rope_tables(pos):
    pos = pos.astype(F32)[:, None]
    lane = np.arange(LANE)

    def build(period, half, active):
        idx = lane % period
        inv = 1.0 / (ROPE_THETA ** (jnp.arange(half, dtype=F32) / half))
        ang = pos * inv[None, :]
        cos = jnp.cos(ang)[:, idx % half]
        sin = jnp.sin(ang)[:, idx % half]
        first = jnp.asarray(active & (idx < half))
        second = jnp.asarray(active & (idx >= half) & (idx < 2 * half))
        c = jnp.where(first | second, cos, 1.0)
        sm = jnp.where(first, -sin, 0.0)
        sp = jnp.where(second, sin, 0.0)
        return c, sm, sp

    a = build(DH_A, ROT_A // 2, np.ones(LANE, bool))
    b = build(LANE, ROPE_B // 2, lane < ROPE_B)
    return a + b


def _project(x, pos, tm, nb, wts, meta=None):
    n = x.shape[0]
    n_pos_tiles = pos.shape[0] // tm
    tiles_per_b = n // nb // tm
    tables = _rope_tables(pos)
    grid = (n // tm,)
    row = lambda w: pl.BlockSpec((tm, w), lambda i: (i, 0))
    full = lambda a: pl.BlockSpec(a.shape, lambda i: (0,) * a.ndim)
    tab = pl.BlockSpec((tm, LANE), lambda i: (i % n_pos_tiles, 0))
    colmajor = pl.BlockSpec((1, HW, tm), lambda i: (i // tiles_per_b, 0, i % tiles_per_b))
    outs = [(HW, BF16), (HW, F32), (HW, BF16), (HW, F32), (KV_LORA, F32), (ROPE_B, F32),
            (HW, BF16), (HW, BF16)]
    out_specs = [row(w) for w, _ in outs] + [colmajor, colmajor]
    t_shape = jax.ShapeDtypeStruct((nb, HW, n // nb), BF16)
    out_shape = [jax.ShapeDtypeStruct((n, w), dt) for w, dt in outs] + [t_shape, t_shape]
    extra_in, extra_specs, scratch, meta_rows = [], [], [], 0
    if meta is not None:
        assert tiles_per_b >= 2
        meta_rows = meta[1].shape[0]
        rows = meta_rows + n // nb
        extra_in, extra_specs = list(meta), [full(a) for a in meta]
        out_specs[3] = out_specs[4] = pl.BlockSpec(memory_space=pl.ANY)
        out_shape[3] = jax.ShapeDtypeStruct((nb, rows * H, DV), F32)
        out_shape[4] = jax.ShapeDtypeStruct((nb, rows, KV_LORA), F32)
        scratch = [pltpu.VMEM((2, tm * H, DV), F32), pltpu.VMEM((2, tm, KV_LORA), F32),
                   pltpu.SemaphoreType.DMA((2, 2)), pltpu.SemaphoreType.DMA((2,))]
    return pl.pallas_call(
        functools.partial(_project_kernel, meta_rows=meta_rows, tiles_per_b=tiles_per_b),
        grid=grid,
        in_specs=[row(D_MODEL)] + [full(a) for a in wts] + [tab] * 6 + extra_specs,
        out_specs=out_specs,
        out_shape=out_shape,
        scratch_shapes=scratch,
        name="project",
        compiler_params=pltpu.CompilerParams(
            dimension_semantics=("arbitrary",), vmem_limit_bytes=VMEM_LIMIT),
    )(x, *wts, *tables, *extra_in)


def _lambda(lq1, lk1, lq2, lk2):
    return (jnp.exp(jnp.sum(lq1[...] * lk1[...], keepdims=True))
            - jnp.exp(jnp.sum(lq2[...] * lk2[...], keepdims=True)) + LAM_INIT)


def _prompt_step(qi, ki, qa_ref, qm_ref, ka_ref, km_ref, vat_ref, vbt_ref,
                 mka_ref, mkm_ref, mvat_ref, mvbt_ref, gcol_ref, lq1, lk1, lq2, lk2,
                 out_ref, qw, ma, acca, mb, accb, *, tq, sk, cw):
    def update(ka, km, vat, vbt, mask):
        streams = [(h, grp, cb) for h in range(H)
                   for grp, ncb in (("a", 2 * tq // cw), ("b", tq // cw)) for cb in range(ncb)]

        def scores(st):
            h, grp, cb = st
            sl = slice(h * LANE, (h + 1) * LANE)
            cols = slice(cb * cw, (cb + 1) * cw)
            if grp == "a":
                return _dot_nt(ka[:, sl], qw[h, cols, :])
            return _dot_nt(km[:, sl], qm_ref[0, cols, sl]) * SCALE_B

        pending = [scores(st) for st in streams[:AHEAD]]
        for i, (h, grp, cb) in enumerate(streams):
            s = pending.pop(0)
            if i + AHEAD < len(streams):
                pending.append(scores(streams[i + AHEAD]))
            sl = slice(h * LANE, (h + 1) * LANE)
            idx = (h, slice(None), slice(cb * cw, (cb + 1) * cw))
            qcol = (cb * cw) % tq
            mk = None if mask is None else mask[:, qcol:qcol + cw]
            m_ref, acc_ref, vt = (ma, acca, vat) if grp == "a" else (mb, accb, vbt)
            if mk is not None:
                s = jnp.where(mk, s, NEG_INF)
            m_old = m_ref[idx]
            m_new = jnp.maximum(m_old, jnp.max(s, axis=0, keepdims=True))
            p = jnp.exp((s - m_new).astype(BF16))
            if mk is not None:
                p = jnp.where(mk, p, 0)
            m_ref[idx] = m_new
            v_ext = jnp.concatenate([vt[sl, :], jnp.ones((ONES_ROWS, vt.shape[1]), BF16)], axis=0)
            acc_ref[idx] = acc_ref[idx] * jnp.exp(m_old - m_new) + _dot(v_ext, p)

    @pl.when(ki == 0)
    def _():
        lane = lax.broadcasted_iota(jnp.int32, (tq, LANE), 1)
        for h in range(H):
            q = qa_ref[0, :, h * LANE:(h + 1) * LANE]
            qw[h, :tq] = jnp.where(lane < DH_A, q, 0)
            qw[h, tq:] = jnp.where(lane >= DH_A, q, 0)
        ma[...] = jnp.full(ma.shape, NEG_INF, F32)
        mb[...] = jnp.full(mb.shape, NEG_INF, F32)
        acca[...] = jnp.zeros(acca.shape, F32)
        accb[...] = jnp.zeros(accb.shape, F32)
        update(mka_ref[...], mkm_ref[...], mvat_ref[0], mvbt_ref[0], None)

    @pl.when(ki < qi)
    def _():
        for j in range(tq // sk):
            rows = slice(j * sk, (j + 1) * sk)
            update(ka_ref[0, rows, :], km_ref[0, rows, :], vat_ref[0, :, rows], vbt_ref[0, :, rows],
                   None)

    @pl.when(ki == qi)
    def _():
        r = lax.broadcasted_iota(jnp.int32, (sk, tq), 0)
        c = lax.broadcasted_iota(jnp.int32, (sk, tq), 1)
        for j in range(tq // sk):
            rows = slice(j * sk, (j + 1) * sk)
            update(ka_ref[0, rows, :], km_ref[0, rows, :], vat_ref[0, :, rows], vbt_ref[0, :, rows],
                   r + j * sk <= c)
        lam = _lambda(lq1, lk1, lq2, lk2)
        for h in range(H):
            o12 = acca[h, :DV] / acca[h, DV:DV + 1]
            d = o12[:, :tq] - lam * o12[:, tq:]
            ms = jnp.mean(d * d, axis=0, keepdims=True)
            o = d * lax.rsqrt(ms + RMS_EPS) * gcol_ref[...] * (1.0 - LAM_INIT)
            out_ref[0, :, h * LANE:(h + 1) * LANE] = o.T.astype(out_ref.dtype)
            ob = accb[h, :DV] / accb[h, DV:DV + 1]
            out_ref[0, :, HW + h * LANE:HW + (h + 1) * LANE] = ob.T.astype(out_ref.dtype)


def _decode_steps(first, pt_ref, qa_ref, qm_ref, kan_ref, van_ref, cn_ref, krn_ref,
                  wabs_ref, wuv_ref, gsub_ref, lq1, lk1, lq2, lk2,
                  kt_hbm, v_hbm, c_hbm, krt_hbm,
                  out_ref, kt_buf, v_buf, c_buf, krt_buf, sem,
                  qbd, qext, ma, la, acca, mb, lb, accb, *, pp, t, sub, n_groups, total):
    last = first + sub - 1
    streams = ((kt_hbm, kt_buf), (v_hbm, v_buf), (c_hbm, c_buf), (krt_hbm, krt_buf))

    def page_copy(kind, page, to_slot, p):
        hbm, buf = streams[kind]
        return pltpu.make_async_copy(hbm.at[page], buf.at[to_slot, p], sem.at[to_slot, kind])

    def fetch(target, to_slot):
        seq, grp = target // n_groups, target % n_groups
        for p in range(pp):
            page = pt_ref[seq, grp * pp + p]
            for kind in range(len(streams)):
                page_copy(kind, page, to_slot, p).start()

    def drain(of_slot):
        for p in range(pp):
            for kind in range(len(streams)):
                page_copy(kind, 0, of_slot, p).wait()

    @pl.when(first == 0)
    def _():
        for slot in range(sub):
            fetch(slot, slot)

    ra, rb = 2 * H * t, H * t
    head_a = (lax.broadcasted_iota(jnp.int32, (ra, LANE), 0) // t) % H

    def rep_rows(q, rows):
        tok = lax.broadcasted_iota(jnp.int32, (rows, HW), 0) % t
        out = jnp.broadcast_to(q[t - 1:t], (rows, HW))
        for i in range(t - 2, -1, -1):
            out = jnp.where(tok == i, jnp.broadcast_to(q[i:i + 1], (rows, HW)), out)
        return out

    @pl.when(first % n_groups == 0)
    def _():
        r = lax.broadcasted_iota(jnp.int32, (ra, HW), 0)
        ln = lax.broadcasted_iota(jnp.int32, (ra, HW), 1)
        chunk = ((r // t) % H) * 2 + r // (H * t)
        qbd[...] = jnp.where(ln // DH_A == chunk, rep_rows(qa_ref[0], ra), 0.0).astype(BF16)
        r = lax.broadcasted_iota(jnp.int32, (rb, HW), 0)
        ln = lax.broadcasted_iota(jnp.int32, (rb, HW), 1)
        q = jnp.where(ln // LANE == r // t, rep_rows(qm_ref[0], rb), 0.0).astype(BF16)
        qext[...] = _dot(q, wabs_ref[...]).astype(BF16)
        ma[...] = jnp.full(ma.shape, NEG_INF, F32)
        mb[...] = jnp.full(mb.shape, NEG_INF, F32)
        la[...] = jnp.zeros(la.shape, F32)
        lb[...] = jnp.zeros(lb.shape, F32)
        acca[...] = jnp.zeros(acca.shape, F32)
        accb[...] = jnp.zeros(accb.shape, F32)

    def pv_a(p, v_of_head):
        p = p.astype(BF16)
        new = jnp.zeros((ra, DV), F32)
        for h in range(0, H, 2):
            pv = _dot(p, jnp.concatenate([v_of_head(h), v_of_head(h + 1)], axis=1))
            new = jnp.where(head_a == h, pv[:, :DV], jnp.where(head_a == h + 1, pv[:, DV:], new))
        return new

    def advance(state, s, mask, pv):
        m_old, l_old, acc = state
        if mask is not None:
            s = jnp.where(mask, s, NEG_INF)
        m_new = jnp.maximum(m_old, jnp.max(s, axis=1, keepdims=True))
        p = jnp.exp(s - m_new)
        if mask is not None:
            p = jnp.where(mask, p, 0.0)
        corr = jnp.exp(m_old - m_new)
        return m_new, l_old * corr + jnp.sum(p, axis=1, keepdims=True), acc * corr + pv(p)

    qe = qext[...]
    st_a = (ma[0], la[0], acca[0])
    st_b = (mb[0], lb[0], accb[0])
    for slot in range(sub):
        drain(slot)
    chains = []
    for slot in range(sub):
        for first_page in range(0, pp, CHAIN_PAGES):
            pages = range(first_page, first_page + CHAIN_PAGES)
            kt = jnp.concatenate([kt_buf[slot, p].astype(BF16) for p in pages], axis=1)
            krt = jnp.concatenate([krt_buf[slot, p].astype(BF16) for p in pages], axis=1)
            c = jnp.concatenate([c_buf[slot, p].astype(BF16) for p in pages], axis=0)
            s_a = _dot(qbd[...], kt)
            s_b = (_dot_nt(qe[:, :KV_LORA], c) + _dot(qe[:, KV_LORA:], krt)) * SCALE_B
            chains.append((slot, pages, s_a, s_b, c))
    for slot, pages, s_a, s_b, c in chains:
        def page_values(h, slot=slot, pages=pages):
            return jnp.concatenate([v_buf[slot, p, pl.ds(h, PAGE, stride=H), :].astype(BF16)
                                    for p in pages], axis=0)

        st_a = advance(st_a, s_a, None, lambda p: pv_a(p, page_values))
        st_b = advance(st_b, s_b, None, lambda p, c=c: _dot(p.astype(BF16), c))
        if pages[-1] == pp - 1:
            fetch((first + sub + slot) % total, slot)
    ma[0], la[0], acca[0] = st_a
    mb[0], lb[0], accb[0] = st_b

    @pl.when(last % n_groups == n_groups - 1)
    def _():
        nk = 16

        def pad_rows(x):
            r = lax.broadcasted_iota(jnp.int32, (nk, x.shape[1]), 0)
            out = jnp.zeros((nk, x.shape[1]), x.dtype)
            for i in range(t):
                out = jnp.where(r == i, jnp.broadcast_to(x[i:i + 1], out.shape), out)
            return out.astype(BF16)

        kn, vn, cn, krn = (pad_rows(r[0]) for r in (kan_ref, van_ref, cn_ref, krn_ref))
        causal_a = (lax.broadcasted_iota(jnp.int32, (ra, nk), 1)
                    <= lax.broadcasted_iota(jnp.int32, (ra, nk), 0) % t)
        causal_b = (lax.broadcasted_iota(jnp.int32, (rb, nk), 1)
                    <= lax.broadcasted_iota(jnp.int32, (rb, nk), 0) % t)
        s_n = (_dot_nt(qe[:, :KV_LORA], cn) + _dot_nt(qe[:, KV_LORA:], krn)) * SCALE_B
        _, l_a, acc_a = advance(st_a, _dot_nt(qbd[...], kn), causal_a,
                                lambda p: pv_a(p, lambda h: vn[:, h * LANE:(h + 1) * LANE]))
        _, l_b, acc_b = advance(st_b, s_n, causal_b, lambda p: _dot(p.astype(BF16), cn))

        lam = _lambda(lq1, lk1, lq2, lk2)
        oa = acc_a / l_a
        o_a = _rms_f32(oa[:rb] - lam * oa[rb:], gsub_ref[...]) * (1.0 - LAM_INIT)
        o_b = _dot((acc_b / l_b).astype(BF16), wuv_ref[...])
        for h in range(H):
            rows = slice(h * t, (h + 1) * t)
            out_ref[0, :, h * LANE:(h + 1) * LANE] = o_a[rows, :]
            out_ref[0, :, HW + h * LANE:HW + (h + 1) * LANE] = o_b[rows, h * LANE:(h + 1) * LANE]

    @pl.when(last == total - 1)
    def _():
        for slot in range(sub):
            drain(slot)


N_PROMPT_IN, N_PROMPT_SCRATCH = 15, 5
N_DECODE_IN, N_DECODE_SCRATCH = 17, 13


def _attend_kernel(pt_ref, *refs, tq, sk, cw, pp, t, sub, n_groups, total):
    prompt_in, refs = refs[:N_PROMPT_IN], refs[N_PROMPT_IN:]
    decode_in, refs = refs[:N_DECODE_IN], refs[N_DECODE_IN:]
    (out_p, out_s), refs = refs[:2], refs[2:]
    prompt_scr, decode_scr = refs[:N_PROMPT_SCRATCH], refs[N_PROMPT_SCRATCH:]
    assert len(decode_scr) == N_DECODE_SCRATCH
    qi, ki = pl.program_id(1), pl.program_id(2)
    step = (pl.program_id(0) * pl.num_programs(1) + qi) * pl.num_programs(2) + ki
    _decode_steps(step * sub, pt_ref, *decode_in, out_s, *decode_scr,
                  pp=pp, t=t, sub=sub, n_groups=n_groups, total=total)
    _prompt_step(qi, ki, *prompt_in, out_p, *prompt_scr, tq=tq, sk=sk, cw=cw)


def _attend(px, pm, ps, caches, page_table, wabs, wuv, g_sub, lams, b, s, db, t, tq, sk, cw, pp):
    qa, _, kab, _, _, _, qm, km, vat, vbt = px
    _, _, mkab, _, _, _, _, mkm, mvat, mvbt = pm
    sqa, skaf, _, svaf, sckv, skr, sqm, _, _, _ = ps
    nq = s // tq
    n_grid = b * nq * nq
    n_groups = page_table.shape[1] // pp
    total = db * n_groups
    sub = total // n_grid
    assert sub * n_grid == total and n_groups % sub == 0
    step_of = lambda bi, qi, ki: (bi * nq + qi) * nq + ki

    r3 = lambda a: a.reshape(b, s, HW)
    qspec = pl.BlockSpec((1, tq, HW), lambda bi, qi, ki, pt: (bi, qi, 0))
    kspec = pl.BlockSpec((1, tq, HW), lambda bi, qi, ki, pt: (bi, jnp.minimum(ki, qi), 0))
    vspec = pl.BlockSpec((1, HW, tq), lambda bi, qi, ki, pt: (bi, 0, jnp.minimum(ki, qi)))
    full = lambda a: pl.BlockSpec(a.shape, lambda bi, qi, ki, pt: (0,) * a.ndim)
    seq3 = lambda a: a.reshape(db, t, a.shape[-1])
    tok = lambda w: pl.BlockSpec((1, t, w),
                                 lambda bi, qi, ki, pt: (step_of(bi, qi, ki) * sub // n_groups, 0, 0))
    gcol = jnp.broadcast_to(g_sub.reshape(DV, 1), (DV, tq))
    prompt_args = [r3(qa), r3(qm), r3(kab), r3(km), vat, vbt, mkab, mkm, mvat, mvbt, gcol] + list(lams)
    prompt_specs = [qspec, qspec, kspec, kspec, vspec, vspec] + [full(a) for a in prompt_args[6:]]
    decode_consts = [wabs, wuv, g_sub] + list(lams)
    decode_args = ([seq3(sqa.astype(F32)), seq3(sqm.astype(F32)), seq3(skaf), seq3(svaf), seq3(sckv),
                    seq3(skr)] + decode_consts + list(caches))
    decode_specs = ([tok(HW), tok(HW), tok(HW), tok(HW), tok(KV_LORA), tok(ROPE_B)]
                    + [full(a) for a in decode_consts]
                    + [pl.BlockSpec(memory_space=pl.ANY)] * len(caches))
    assert len(prompt_args) == N_PROMPT_IN and len(decode_args) == N_DECODE_IN
    ra, rb = 2 * t * H, t * H
    prompt_scratch = [pltpu.VMEM((H, 2 * tq, LANE), BF16),
                      pltpu.VMEM((H, 1, 2 * tq), F32), pltpu.VMEM((H, DV + ONES_ROWS, 2 * tq), F32),
                      pltpu.VMEM((H, 1, tq), F32), pltpu.VMEM((H, DV + ONES_ROWS, tq), F32)]
    decode_scratch = ([pltpu.VMEM((sub, pp) + a.shape[1:], a.dtype) for a in caches]
                      + [pltpu.SemaphoreType.DMA((sub, len(caches))),
                         pltpu.VMEM((ra, HW), BF16), pltpu.VMEM((rb, KV_LORA + ROPE_B), BF16),
                         pltpu.VMEM((1, ra, 1), F32), pltpu.VMEM((1, ra, 1), F32),
                         pltpu.VMEM((1, ra, DV), F32),
                         pltpu.VMEM((1, rb, 1), F32), pltpu.VMEM((1, rb, 1), F32),
                         pltpu.VMEM((1, rb, KV_LORA), F32)])
    assert len(prompt_scratch) == N_PROMPT_SCRATCH and len(decode_scratch) == N_DECODE_SCRATCH
    grid_spec = pltpu.PrefetchScalarGridSpec(
        num_scalar_prefetch=1,
        grid=(b, nq, nq),
        in_specs=prompt_specs + decode_specs,
        out_specs=[pl.BlockSpec((1, tq, 2 * HW), lambda bi, qi, ki, pt: (bi, qi, 0)),
                   pl.BlockSpec((1, t, 2 * HW),
                                lambda bi, qi, ki, pt: (step_of(bi, qi, ki) * sub // n_groups, 0, 0))],
        scratch_shapes=prompt_scratch + decode_scratch)
    return pl.pallas_call(
        functools.partial(_attend_kernel, tq=tq, sk=sk, cw=cw, pp=pp, t=t, sub=sub,
                          n_groups=n_groups, total=total),
        grid_spec=grid_spec,
        out_shape=[jax.ShapeDtypeStruct((b, s, 2 * HW), BF16),
                   jax.ShapeDtypeStruct((db, t, 2 * HW), F32)],
        name="attend",
        compiler_params=pltpu.CompilerParams(
            dimension_semantics=("arbitrary", "arbitrary", "arbitrary"),
            vmem_limit_bytes=VMEM_LIMIT),
    )(page_table, *prompt_args, *decode_args)


def _mix_ffn_kernel(x_ref, hd_ref, wout_ref, wup_ref, wdn_ref, gpm_ref, gpf_ref, gqf_ref, y_ref):
    mix = _dot(hd_ref[...].astype(BF16), wout_ref[...])
    x1 = x_ref[...] + _rms_f32(mix, gpm_ref[...])
    h = _rms_f32(x1, gpf_ref[...]).astype(BF16)
    u = jnp.maximum(_dot(h, wup_ref[...]), 0.0)
    f = _dot((u * u).astype(BF16), wdn_ref[...])
    y_ref[...] = x1 + _rms_f32(f, gqf_ref[...])


def _mix_ffn(x, heads, w_out, w_up, w_down, g_post_mix, g_pre_ffn, g_post_ffn, tm):
    n = x.shape[0]
    tm = min(tm, n)
    assert n % tm == 0
    row = lambda w: pl.BlockSpec((tm, w), lambda i: (i, 0))
    const = lambda a: pl.BlockSpec(a.shape, lambda i: (0,) * a.ndim, pipeline_mode=pl.Buffered(1))
    consts = [w_out, w_up, w_down, g_post_mix, g_pre_ffn, g_post_ffn]
    return pl.pallas_call(
        _mix_ffn_kernel,
        grid=(n // tm,),
        in_specs=[row(D_MODEL), row(2 * HW)] + [const(a) for a in consts],
        out_specs=row(D_MODEL),
        out_shape=jax.ShapeDtypeStruct((n, D_MODEL), F32),
        name="mix_ffn",
        compiler_params=pltpu.CompilerParams(
            dimension_semantics=("arbitrary",), vmem_limit_bytes=VMEM_LIMIT),
    )(x, heads, *consts)


def _prep_weights(w_in, w_q_b, w_kv_b):
    w_in_p = jnp.pad(w_in, ((0, 0), (0, IN_COLS_PAD - IN_COLS))).astype(BF16)
    w_va_t = w_in[:, C_VA:C_VA + HW].T.astype(BF16)
    w_qb_p = jnp.pad(w_q_b, ((0, 0), (0, 0), (0, LANE - NOPE_B - ROPE_B))).reshape(Q_LORA, HW)
    w_uk = w_kv_b[..., :NOPE_B]
    w_uv = w_kv_b[..., NOPE_B:]
    w_uk_p = jnp.pad(w_uk, ((0, 0), (0, 0), (0, LANE - NOPE_B))).reshape(KV_LORA, HW)
    w_uv_f = w_uv.reshape(KV_LORA, HW)
    eye_r = jnp.eye(ROPE_B, dtype=w_in.dtype)
    per_head = []
    for h in range(H):
        lat = jnp.concatenate([w_uk[:, h, :].T, jnp.zeros((LANE - NOPE_B, KV_LORA), w_in.dtype)], 0)
        rope = jnp.concatenate([jnp.zeros((NOPE_B, ROPE_B), w_in.dtype), eye_r,
                                jnp.zeros((LANE - NOPE_B - ROPE_B, ROPE_B), w_in.dtype)], 0)
        per_head.append(jnp.concatenate([lat, rope], axis=1))
    w_abs = jnp.concatenate(per_head, axis=0)
    return (w_in_p, w_qb_p.astype(BF16), w_uk_p.astype(BF16), w_va_t, w_uv_f.T.astype(BF16),
            w_abs.astype(BF16), w_uv_f.astype(BF16))


def kernel(x_prompt, x_sample, cache_diff_k, cache_diff_v, cache_mla_latent, cache_mla_krope,
           page_table, meta_tokens, g_pre_mix, g_post_mix, g_pre_ffn, g_post_ffn, w_in, g_q_a,
           w_q_b, g_kv_a, w_kv_b, lambda_q1, lambda_k1, lambda_q2, lambda_k2, g_sub, w_out,
           w_up, w_down):
    b, s, d = x_prompt.shape
    db, t, _ = x_sample.shape
    n_pool = cache_diff_k.shape[1]
    past = page_table.shape[1] * PAGE
    assert w_in.shape[0] == 1, "single-layer trunk"
    layer = 0

    w_in_p, w_qb_p, w_uk_p, w_va_t, w_uv_t, w_abs, w_uv_f = _prep_weights(
        w_in[layer], w_q_b[layer], w_kv_b[layer])
    proj_w = (g_pre_mix, w_in_p, g_q_a, w_qb_p, g_kv_a, w_uk_p, w_va_t, w_uv_t)
    lams = (lambda_q1, lambda_k1, lambda_q2, lambda_k2)

    tm = 512
    pm = _project(meta_tokens.astype(x_prompt.dtype), jnp.arange(N_META), N_META, 1, proj_w)
    meta_v = pm[3].reshape(N_META * H, DV)
    px = _project(x_prompt.reshape(b * s, d), N_META + jnp.arange(s), tm, b, proj_w, meta=(meta_v, pm[4]))
    tms = min(tm, db * t)
    ps = _project(x_sample.reshape(db * t, d), past + (jnp.arange(tms) % t), tms, 1, proj_w)

    caches = (jnp.transpose(cache_diff_k[layer], (0, 2, 3, 4, 1)).reshape(n_pool, HW, PAGE),
              cache_diff_v[layer].reshape(n_pool, PAGE * H, DV),
              cache_mla_latent[layer],
              jnp.transpose(cache_mla_krope[layer], (0, 2, 1)))
    heads_p, heads_s = _attend(px, pm, ps, caches, page_table, w_abs, w_uv_f, g_sub, lams,
                               b, s, db, t, tq=512, sk=512, cw=512, pp=8)

    ffn_w = (w_out[layer].astype(BF16), w_up[layer].astype(BF16), w_down[layer].astype(BF16),
             g_post_mix, g_pre_ffn, g_post_ffn)
    y_prompt = _mix_ffn(x_prompt.reshape(b * s, d), heads_p.reshape(b * s, 2 * HW), *ffn_w, tm=512)
    y_sample = _mix_ffn(x_sample.reshape(db * t, d), heads_s.reshape(db * t, 2 * HW), *ffn_w, tm=512)

    def with_meta(x_part, m_part, tail):
        m = jnp.broadcast_to(m_part[None], (b, N_META, m_part.shape[-1]))
        full = jnp.concatenate([m, x_part.reshape(b, s, -1)], axis=1)
        return full.reshape((1, b, s + N_META) + tail)

    k_shape, v_shape = (H, 2, DH_A), (H, DV)
    return (y_prompt.reshape(b, s, d), y_sample.reshape(db, t, d),
            with_meta(px[1], pm[1], k_shape), px[3].reshape((1, b, s + N_META) + v_shape),
            px[4][None], with_meta(px[5], pm[5], (ROPE_B,)),
            ps[1].reshape((1, db, t) + k_shape), ps[3].reshape((1, db, t) + v_shape),
            ps[4].reshape(1, db, t, KV_LORA), ps[5].reshape(1, db, t, ROPE_B))
```

```python
import functools
import math

import jax
import jax.numpy as jnp
import numpy as np
from jax import lax
from jax.experimental import pallas as pl
from jax.experimental.pallas import tpu as pltpu

D_MODEL = 1024
N_META = 16
H = 4
DH_A = 64
DV = 128
ROT_A = 16
Q_LORA = 384
KV_LORA = 256
NOPE_B = 64
ROPE_B = 32
D_FF = 4096
PAGE = 128
ROPE_THETA = 500000.0
RMS_EPS = 1e-6
NEG_INF = -1e30
SCALE_A = DH_A ** -0.5
SCALE_B = (NOPE_B + ROPE_B) ** -0.5
LAM_INIT = 0.8 - 0.6 * math.exp(-0.3 * 0)

LANE = 128
AHEAD = 2
SCORE_KINDS, VALUE_KINDS = (0, 2, 3), (1,)
CHAIN_PAGES = 4
ONES_ROWS = 16
HW = H * LANE
IN_COLS = 2208
IN_COLS_PAD = 2304
C_QA, C_KA, C_VA, C_QL, C_CKV, C_KR = 0, 512, 1024, 1536, 1920, 2176

VMEM_LIMIT = 48 * 1024 * 1024

F32 = jnp.float32
BF16 = jnp.bfloat16


def _rms_f32(x, g):
    return x * lax.rsqrt(jnp.mean(x * x, axis=-1, keepdims=True) + RMS_EPS) * g


def _rot(x, c, sm, sp, half):
    w = x.shape[-1]
    return x * c + pltpu.roll(x, w - half, 1) * sm + pltpu.roll(x, half, 1) * sp


def _dot(a, b):
    return jnp.dot(a, b, preferred_element_type=F32)


def _dot_nt(a, b):
    return lax.dot_general(a, b, (((1,), (1,)), ((), ())), preferred_element_type=F32)


def _project_kernel(x_ref, g_ref, win_ref, gq_ref, wqb_ref, gkv_ref, wuk_ref, wvat_ref, wuvt_ref,
                    ca_ref, sma_ref, spa_ref, cb_ref, smb_ref, spb_ref, *rest, meta_rows, n_batch):
    if meta_rows:
        mv_ref, mc_ref = rest[:2]
        qa_o, kaf_o, kab_o, vaf_o, ckv_o, kr_o, qm_o, km_o, vat_o, vbt_o, vbuf, cbuf, sem, msem = rest[2:]
    else:
        qa_o, kaf_o, kab_o, vaf_o, ckv_o, kr_o, qm_o, km_o, vat_o, vbt_o = rest
    h = _rms_f32(x_ref[...], g_ref[...]).astype(BF16)
    zl = _dot(h, win_ref[:, C_QL:])
    z = _dot(h, win_ref[:, :C_QL])

    ckv = _rms_f32(zl[:, C_CKV - C_QL:C_KR - C_QL], gkv_ref[...])
    ckv_b = ckv.astype(BF16)
    ql = _rms_f32(zl[:, :Q_LORA], gq_ref[...]).astype(BF16)
    kn = _dot(ckv_b, wuk_ref[...])
    qb = _dot(ql, wqb_ref[...])
    vat_o[0] = _dot_nt(wvat_ref[...], h).astype(BF16)
    vbt_o[0] = _dot_nt(wuvt_ref[...], ckv_b).astype(BF16)

    ca, sma, spa = ca_ref[...], sma_ref[...], spa_ref[...]
    for j in range(H):
        sl = slice(j * LANE, (j + 1) * LANE)
        q = _rot(z[:, C_QA + j * LANE:C_QA + (j + 1) * LANE], ca, sma, spa, ROT_A // 2)
        k = _rot(z[:, C_KA + j * LANE:C_KA + (j + 1) * LANE], ca, sma, spa, ROT_A // 2)
        qa_o[:, sl] = (q * SCALE_A).astype(BF16)
        kaf_o[:, sl] = k
        kab_o[:, sl] = k.astype(BF16)

    cb, smb, spb = cb_ref[...], smb_ref[...], spb_ref[...]
    kr = _rot(zl[:, C_KR - C_QL:], cb, smb, spb, ROPE_B // 2)
    kr_o[...] = kr[:, :ROPE_B]
    kr_at64 = pltpu.roll(kr, NOPE_B, 1)

    cq = pltpu.roll(cb, NOPE_B, 1)
    smq = pltpu.roll(smb, NOPE_B, 1)
    spq = pltpu.roll(spb, NOPE_B, 1)
    for j in range(H):
        sl = slice(j * LANE, (j + 1) * LANE)
        qm_o[:, sl] = _rot(qb[:, sl], cq, smq, spq, ROPE_B // 2).astype(BF16)
        km_o[:, sl] = (kn[:, sl] + kr_at64).astype(BF16)

    if meta_rows:
        i = pl.program_id(0)
        tm = x_ref.shape[0]
        bi, ti, slot = i % n_batch, i // n_batch, i % 2

        def row_copies(to_slot, b_idx, t_idx):
            return (pltpu.make_async_copy(
                        vbuf.at[to_slot], vaf_o.at[b_idx, pl.ds((meta_rows + t_idx * tm) * H, tm * H), :],
                        sem.at[to_slot, 0]),
                    pltpu.make_async_copy(
                        cbuf.at[to_slot], ckv_o.at[b_idx, pl.ds(meta_rows + t_idx * tm, tm), :],
                        sem.at[to_slot, 1]))

        def meta_copies(b_idx):
            return (pltpu.make_async_copy(mv_ref, vaf_o.at[b_idx, pl.ds(0, meta_rows * H), :], msem.at[0]),
                    pltpu.make_async_copy(mc_ref, ckv_o.at[b_idx, pl.ds(0, meta_rows), :], msem.at[1]))

        @pl.when(i >= 2)
        def _():
            for cp in row_copies(slot, 0, 0):
                cp.wait()

        for j in range(H):
            vbuf[slot, pl.ds(j, tm, stride=H), :] = z[:, C_VA + j * LANE:C_VA + (j + 1) * LANE]
        cbuf[slot] = ckv
        for cp in row_copies(slot, bi, ti):
            cp.start()

        @pl.when(ti == 0)
        def _():
            for cp in meta_copies(bi):
                cp.start()

        @pl.when(ti == 1)
        def _():
            for cp in meta_copies(bi):
                cp.wait()

        @pl.when(i == pl.num_programs(0) - 1)
        def _():
            for cp in row_copies(1 - slot, 0, 0) + row_copies(slot, 0, 0):
                cp.wait()
    else:
        vaf_o[...] = z[:, C_VA:C_VA + HW]
        ckv_o[...] = ckv


def _rope_tables(pos):
    pos = pos.astype(F32)[:, None]
    lane = np.arange(LANE)

    def build(period, half, active):
        idx = lane % period
        inv = 1.0 / (ROPE_THETA ** (jnp.arange(half, dtype=F32) / half))
        ang = pos * inv[None, :]
        cos = jnp.cos(ang)[:, idx % half]
        sin = jnp.sin(ang)[:, idx % half]
        first = jnp.asarray(active & (idx < half))
        second = jnp.asarray(active & (idx >= half) & (idx < 2 * half))
        c = jnp.where(first | second, cos, 1.0)
        sm = jnp.where(first, -sin, 0.0)
        sp = jnp.where(second, sin, 0.0)
        return c, sm, sp

    a = build(DH_A, ROT_A // 2, np.ones(LANE, bool))
    b = build(LANE, ROPE_B // 2, lane < ROPE_B)
    return a + b


def _project(x, pos, tm, nb, wts, meta=None):
    n = x.shape[0]
    n_pos_tiles = pos.shape[0] // tm
    tiles_per_b = n // nb // tm
    tables = _rope_tables(pos)
    grid = (n // tm,)
    row = lambda w: pl.BlockSpec((tm, w), lambda i: ((i % nb) * tiles_per_b + i // nb, 0))
    full = lambda a: pl.BlockSpec(a.shape, lambda i: (0,) * a.ndim)
    tab = pl.BlockSpec((tm, LANE), lambda i: ((i // nb) % n_pos_tiles, 0))
    colmajor = pl.BlockSpec((1, HW, tm), lambda i: (i % nb, 0, i // nb))
    outs = [(HW, BF16), (HW, F32), (HW, BF16), (HW, F32), (KV_LORA, F32), (ROPE_B, F32),
            (HW, BF16), (HW, BF16)]
    out_specs = [row(w) for w, _ in outs] + [colmajor, colmajor]
    t_shape = jax.ShapeDtypeStruct((nb, HW, n // nb), BF16)
    out_shape = [jax.ShapeDtypeStruct((n, w), dt) for w, dt in outs] + [t_shape, t_shape]
    extra_in, extra_specs, scratch, meta_rows = [], [], [], 0
    if meta is not None:
        assert tiles_per_b >= 2
        meta_rows = meta[1].shape[0]
        rows = meta_rows + n // nb
        extra_in, extra_specs = list(meta), [full(a) for a in meta]
        out_specs[3] = out_specs[4] = pl.BlockSpec(memory_space=pl.ANY)
        out_shape[3] = jax.ShapeDtypeStruct((nb, rows * H, DV), F32)
        out_shape[4] = jax.ShapeDtypeStruct((nb, rows, KV_LORA), F32)
        scratch = [pltpu.VMEM((2, tm * H, DV), F32), pltpu.VMEM((2, tm, KV_LORA), F32),
                   pltpu.SemaphoreType.DMA((2, 2)), pltpu.SemaphoreType.DMA((2,))]
    return pl.pallas_call(
        functools.partial(_project_kernel, meta_rows=meta_rows, n_batch=nb),
        grid=grid,
        in_specs=[row(D_MODEL)] + [full(a) for a in wts] + [tab] * 6 + extra_specs,
        out_specs=out_specs,
        out_shape=out_shape,
        scratch_shapes=scratch,
        name="project",
        compiler_params=pltpu.CompilerParams(
            dimension_semantics=("arbitrary",), vmem_limit_bytes=VMEM_LIMIT),
    )(x, *wts, *tables, *extra_in)


def _lambda(lq1, lk1, lq2, lk2):
    return (jnp.exp(jnp.sum(lq1[...] * lk1[...], keepdims=True))
            - jnp.exp(jnp.sum(lq2[...] * lk2[...], keepdims=True)) + LAM_INIT)


def _prompt_step(qi, ki, qa_ref, qm_ref, ka_ref, km_ref, vat_ref, vbt_ref,
                 mka_ref, mkm_ref, mvat_ref, mvbt_ref, gcol_ref, lq1, lk1, lq2, lk2,
                 out_ref, qw, ma, acca, mb, accb, *, tq, sk, cw):
    def update(ka, km, vat, vbt, mask):
        streams = [(h, grp, cb) for h in range(H)
                   for grp, ncb in (("a", 2 * tq // cw), ("b", tq // cw)) for cb in range(ncb)]

        def scores(st):
            h, grp, cb = st
            sl = slice(h * LANE, (h + 1) * LANE)
            cols = slice(cb * cw, (cb + 1) * cw)
            if grp == "a":
                return _dot_nt(ka[:, sl], qw[h, cols, :])
            return _dot_nt(km[:, sl], qm_ref[0, cols, sl]) * SCALE_B

        pending = [scores(st) for st in streams[:AHEAD]]
        for i, (h, grp, cb) in enumerate(streams):
            s = pending.pop(0)
            if i + AHEAD < len(streams):
                pending.append(scores(streams[i + AHEAD]))
            sl = slice(h * LANE, (h + 1) * LANE)
            idx = (h, slice(None), slice(cb * cw, (cb + 1) * cw))
            qcol = (cb * cw) % tq
            mk = None if mask is None else mask[:, qcol:qcol + cw]
            m_ref, acc_ref, vt = (ma, acca, vat) if grp == "a" else (mb, accb, vbt)
            if mk is not None:
                s = jnp.where(mk, s, NEG_INF)
            m_old = m_ref[idx]
            m_new = jnp.maximum(m_old, jnp.max(s, axis=0, keepdims=True))
            p = jnp.exp((s - m_new).astype(BF16))
            if mk is not None:
                p = jnp.where(mk, p, 0)
            m_ref[idx] = m_new
            v_ext = jnp.concatenate([vt[sl, :], jnp.ones((ONES_ROWS, vt.shape[1]), BF16)], axis=0)
            acc_ref[idx] = acc_ref[idx] * jnp.exp(m_old - m_new) + _dot(v_ext, p)

    @pl.when(ki == 0)
    def _():
        lane = lax.broadcasted_iota(jnp.int32, (tq, LANE), 1)
        for h in range(H):
            q = qa_ref[0, :, h * LANE:(h + 1) * LANE]
            qw[h, :tq] = jnp.where(lane < DH_A, q, 0)
            qw[h, tq:] = jnp.where(lane >= DH_A, q, 0)
        ma[...] = jnp.full(ma.shape, NEG_INF, F32)
        mb[...] = jnp.full(mb.shape, NEG_INF, F32)
        acca[...] = jnp.zeros(acca.shape, F32)
        accb[...] = jnp.zeros(accb.shape, F32)
        update(mka_ref[...], mkm_ref[...], mvat_ref[0], mvbt_ref[0], None)

    @pl.when(ki < qi)
    def _():
        for j in range(tq // sk):
            rows = slice(j * sk, (j + 1) * sk)
            update(ka_ref[0, rows, :], km_ref[0, rows, :], vat_ref[0, :, rows], vbt_ref[0, :, rows],
                   None)

    @pl.when(ki == qi)
    def _():
        r = lax.broadcasted_iota(jnp.int32, (sk, tq), 0)
        c = lax.broadcasted_iota(jnp.int32, (sk, tq), 1)
        for j in range(tq // sk):
            rows = slice(j * sk, (j + 1) * sk)
            update(ka_ref[0, rows, :], km_ref[0, rows, :], vat_ref[0, :, rows], vbt_ref[0, :, rows],
                   r + j * sk <= c)
        lam = _lambda(lq1, lk1, lq2, lk2)
        for h in range(H):
            o12 = acca[h, :DV] / acca[h, DV:DV + 1]
            d = o12[:, :tq] - lam * o12[:, tq:]
            ms = jnp.mean(d * d, axis=0, keepdims=True)
            o = d * lax.rsqrt(ms + RMS_EPS) * gcol_ref[...] * (1.0 - LAM_INIT)
            out_ref[0, :, h * LANE:(h + 1) * LANE] = o.T.astype(out_ref.dtype)
            ob = accb[h, :DV] / accb[h, DV:DV + 1]
            out_ref[0, :, HW + h * LANE:HW + (h + 1) * LANE] = ob.T.astype(out_ref.dtype)


def _decode_steps(first, pt_ref, qa_ref, qm_ref, kan_ref, van_ref, cn_ref, krn_ref,
                  wabs_ref, wuv_ref, gsub_ref, lq1, lk1, lq2, lk2,
                  kt_hbm, v_hbm, c_hbm, krt_hbm,
                  out_ref, kt_buf, v_buf, c_buf, krt_buf, sem,
                  qbd, qext, ma, la, acca, mb, lb, accb, *, pp, t, sub, n_groups, total):
    last = first + sub - 1
    streams = ((kt_hbm, kt_buf), (v_hbm, v_buf), (c_hbm, c_buf), (krt_hbm, krt_buf))

    def page_copy(kind, page, to_slot, p):
        hbm, buf = streams[kind]
        return pltpu.make_async_copy(hbm.at[page], buf.at[to_slot, p], sem.at[to_slot, kind])

    def fetch(target, to_slot, kinds):
        seq, grp = target // n_groups, target % n_groups
        for p in range(pp):
            page = pt_ref[seq, grp * pp + p]
            for kind in kinds:
                page_copy(kind, page, to_slot, p).start()

    def drain(of_slot):
        for p in range(pp):
            for kind in range(len(streams)):
                page_copy(kind, 0, of_slot, p).wait()

    @pl.when(first == 0)
    def _():
        for slot in range(sub):
            fetch(slot, slot, range(len(streams)))

    ra, rb = 2 * H * t, H * t
    head_a = (lax.broadcasted_iota(jnp.int32, (ra, LANE), 0) // t) % H

    def rep_rows(q, rows):
        tok = lax.broadcasted_iota(jnp.int32, (rows, HW), 0) % t
        out = jnp.broadcast_to(q[t - 1:t], (rows, HW))
        for i in range(t - 2, -1, -1):
            out = jnp.where(tok == i, jnp.broadcast_to(q[i:i + 1], (rows, HW)), out)
        return out

    @pl.when(first % n_groups == 0)
    def _():
        r = lax.broadcasted_iota(jnp.int32, (ra, HW), 0)
        ln = lax.broadcasted_iota(jnp.int32, (ra, HW), 1)
        chunk = ((r // t) % H) * 2 + r // (H * t)
        qbd[...] = jnp.where(ln // DH_A == chunk, rep_rows(qa_ref[0], ra), 0.0).astype(BF16)
        r = lax.broadcasted_iota(jnp.int32, (rb, HW), 0)
        ln = lax.broadcasted_iota(jnp.int32, (rb, HW), 1)
        q = jnp.where(ln // LANE == r // t, rep_rows(qm_ref[0], rb), 0.0).astype(BF16)
        qext[...] = _dot(q, wabs_ref[...]).astype(BF16)
        ma[...] = jnp.full(ma.shape, NEG_INF, F32)
        mb[...] = jnp.full(mb.shape, NEG_INF, F32)
        la[...] = jnp.zeros(la.shape, F32)
        lb[...] = jnp.zeros(lb.shape, F32)
        acca[...] = jnp.zeros(acca.shape, F32)
        accb[...] = jnp.zeros(accb.shape, F32)

    def pv_a(p, v_of_head):
        p = p.astype(BF16)
        new = jnp.zeros((ra, DV), F32)
        for h in range(0, H, 2):
            pv = _dot(p, jnp.concatenate([v_of_head(h), v_of_head(h + 1)], axis=1))
            new = jnp.where(head_a == h, pv[:, :DV], jnp.where(head_a == h + 1, pv[:, DV:], new))
        return new

    def advance(state, s, mask, pv):
        m_old, l_old, acc = state
        if mask is not None:
            s = jnp.where(mask, s, NEG_INF)
        m_new = jnp.maximum(m_old, jnp.max(s, axis=1, keepdims=True))
        p = jnp.exp(s - m_new)
        if mask is not None:
            p = jnp.where(mask, p, 0.0)
        corr = jnp.exp(m_old - m_new)
        return m_new, l_old * corr + jnp.sum(p, axis=1, keepdims=True), acc * corr + pv(p)

    qe = qext[...]
    st_a = (ma[0], la[0], acca[0])
    st_b = (mb[0], lb[0], accb[0])
    for slot in range(sub):
        drain(slot)
    chains = []
    for slot in range(sub):
        for first_page in range(0, pp, CHAIN_PAGES):
            pages = range(first_page, first_page + CHAIN_PAGES)
            kt = jnp.concatenate([kt_buf[slot, p].astype(BF16) for p in pages], axis=1)
            krt = jnp.concatenate([krt_buf[slot, p].astype(BF16) for p in pages], axis=1)
            c = jnp.concatenate([c_buf[slot, p].astype(BF16) for p in pages], axis=0)
            s_a = _dot(qbd[...], kt)
            s_b = (_dot_nt(qe[:, :KV_LORA], c) + _dot(qe[:, KV_LORA:], krt)) * SCALE_B
            chains.append((slot, pages, s_a, s_b, c))
        fetch((first + sub + slot) % total, slot, SCORE_KINDS)
    for slot, pages, s_a, s_b, c in chains:
        def page_values(h, slot=slot, pages=pages):
            return jnp.concatenate([v_buf[slot, p, pl.ds(h, PAGE, stride=H), :].astype(BF16)
                                    for p in pages], axis=0)

        st_a = advance(st_a, s_a, None, lambda p: pv_a(p, page_values))
        st_b = advance(st_b, s_b, None, lambda p, c=c: _dot(p.astype(BF16), c))
        if pages[-1] == pp - 1:
            fetch((first + sub + slot) % total, slot, VALUE_KINDS)
    ma[0], la[0], acca[0] = st_a
    mb[0], lb[0], accb[0] = st_b

    @pl.when(last % n_groups == n_groups - 1)
    def _():
        nk = 16

        def pad_rows(x):
            r = lax.broadcasted_iota(jnp.int32, (nk, x.shape[1]), 0)
            out = jnp.zeros((nk, x.shape[1]), x.dtype)
            for i in range(t):
                out = jnp.where(r == i, jnp.broadcast_to(x[i:i + 1], out.shape), out)
            return out.astype(BF16)

        kn, vn, cn, krn = (pad_rows(r[0]) for r in (kan_ref, van_ref, cn_ref, krn_ref))
        causal_a = (lax.broadcasted_iota(jnp.int32, (ra, nk), 1)
                    <= lax.broadcasted_iota(jnp.int32, (ra, nk), 0) % t)
        causal_b = (lax.broadcasted_iota(jnp.int32, (rb, nk), 1)
                    <= lax.broadcasted_iota(jnp.int32, (rb, nk), 0) % t)
        s_n = (_dot_nt(qe[:, :KV_LORA], cn) + _dot_nt(qe[:, KV_LORA:], krn)) * SCALE_B
        _, l_a, acc_a = advance(st_a, _dot_nt(qbd[...], kn), causal_a,
                                lambda p: pv_a(p, lambda h: vn[:, h * LANE:(h + 1) * LANE]))
        _, l_b, acc_b = advance(st_b, s_n, causal_b, lambda p: _dot(p.astype(BF16), cn))

        lam = _lambda(lq1, lk1, lq2, lk2)
        oa = acc_a / l_a
        o_a = _rms_f32(oa[:rb] - lam * oa[rb:], gsub_ref[...]) * (1.0 - LAM_INIT)
        o_b = _dot((acc_b / l_b).astype(BF16), wuv_ref[...])
        for h in range(H):
            rows = slice(h * t, (h + 1) * t)
            out_ref[0, :, h * LANE:(h + 1) * LANE] = o_a[rows, :]
            out_ref[0, :, HW + h * LANE:HW + (h + 1) * LANE] = o_b[rows, h * LANE:(h + 1) * LANE]

    @pl.when(last == total - 1)
    def _():
        for slot in range(sub):
            drain(slot)


N_PROMPT_IN, N_PROMPT_SCRATCH = 15, 5
N_DECODE_IN, N_DECODE_SCRATCH = 17, 13


def _attend_kernel(pt_ref, *refs, tq, sk, cw, pp, t, sub, n_groups, total):
    prompt_in, refs = refs[:N_PROMPT_IN], refs[N_PROMPT_IN:]
    decode_in, refs = refs[:N_DECODE_IN], refs[N_DECODE_IN:]
    (out_p, out_s), refs = refs[:2], refs[2:]
    prompt_scr, decode_scr = refs[:N_PROMPT_SCRATCH], refs[N_PROMPT_SCRATCH:]
    assert len(decode_scr) == N_DECODE_SCRATCH
    qi, ki = pl.program_id(1), pl.program_id(2)
    step = (pl.program_id(0) * pl.num_programs(1) + qi) * pl.num_programs(2) + ki
    _decode_steps(step * sub, pt_ref, *decode_in, out_s, *decode_scr,
                  pp=pp, t=t, sub=sub, n_groups=n_groups, total=total)
    _prompt_step(qi, ki, *prompt_in, out_p, *prompt_scr, tq=tq, sk=sk, cw=cw)


def _attend(px, pm, ps, caches, page_table, wabs, wuv, g_sub, lams, b, s, db, t, tq, sk, cw, pp):
    qa, _, kab, _, _, _, qm, km, vat, vbt = px
    _, _, mkab, _, _, _, _, mkm, mvat, mvbt = pm
    sqa, skaf, _, svaf, sckv, skr, sqm, _, _, _ = ps
    nq = s // tq
    n_grid = b * nq * nq
    n_groups = page_table.shape[1] // pp
    total = db * n_groups
    sub = total // n_grid
    assert sub * n_grid == total and n_groups % sub == 0
    step_of = lambda bi, qi, ki: (bi * nq + qi) * nq + ki

    r3 = lambda a: a.reshape(b, s, HW)
    qspec = pl.BlockSpec((1, tq, HW), lambda bi, qi, ki, pt: (bi, qi, 0))
    kspec = pl.BlockSpec((1, tq, HW), lambda bi, qi, ki, pt: (bi, jnp.minimum(ki, qi), 0))
    vspec = pl.BlockSpec((1, HW, tq), lambda bi, qi, ki, pt: (bi, 0, jnp.minimum(ki, qi)))
    full = lambda a: pl.BlockSpec(a.shape, lambda bi, qi, ki, pt: (0,) * a.ndim)
    seq3 = lambda a: a.reshape(db, t, a.shape[-1])
    tok = lambda w: pl.BlockSpec((1, t, w),
                                 lambda bi, qi, ki, pt: (step_of(bi, qi, ki) * sub // n_groups, 0, 0))
    gcol = jnp.broadcast_to(g_sub.reshape(DV, 1), (DV, tq))
    prompt_args = [r3(qa), r3(qm), r3(kab), r3(km), vat, vbt, mkab, mkm, mvat, mvbt, gcol] + list(lams)
    prompt_specs = [qspec, qspec, kspec, kspec, vspec, vspec] + [full(a) for a in prompt_args[6:]]
    decode_consts = [wabs, wuv, g_sub] + list(lams)
    decode_args = ([seq3(sqa.astype(F32)), seq3(sqm.astype(F32)), seq3(skaf), seq3(svaf), seq3(sckv),
                    seq3(skr)] + decode_consts + list(caches))
    decode_specs = ([tok(HW), tok(HW), tok(HW), tok(HW), tok(KV_LORA), tok(ROPE_B)]
                    + [full(a) for a in decode_consts]
                    + [pl.BlockSpec(memory_space=pl.ANY)] * len(caches))
    assert len(prompt_args) == N_PROMPT_IN and len(decode_args) == N_DECODE_IN
    ra, rb = 2 * t * H, t * H
    prompt_scratch = [pltpu.VMEM((H, 2 * tq, LANE), BF16),
                      pltpu.VMEM((H, 1, 2 * tq), F32), pltpu.VMEM((H, DV + ONES_ROWS, 2 * tq), F32),
                      pltpu.VMEM((H, 1, tq), F32), pltpu.VMEM((H, DV + ONES_ROWS, tq), F32)]
    decode_scratch = ([pltpu.VMEM((sub, pp) + a.shape[1:], a.dtype) for a in caches]
                      + [pltpu.SemaphoreType.DMA((sub, len(caches))),
                         pltpu.VMEM((ra, HW), BF16), pltpu.VMEM((rb, KV_LORA + ROPE_B), BF16),
                         pltpu.VMEM((1, ra, 1), F32), pltpu.VMEM((1, ra, 1), F32),
                         pltpu.VMEM((1, ra, DV), F32),
                         pltpu.VMEM((1, rb, 1), F32), pltpu.VMEM((1, rb, 1), F32),
                         pltpu.VMEM((1, rb, KV_LORA), F32)])
    assert len(prompt_scratch) == N_PROMPT_SCRATCH and len(decode_scratch) == N_DECODE_SCRATCH
    grid_spec = pltpu.PrefetchScalarGridSpec(
        num_scalar_prefetch=1,
        grid=(b, nq, nq),
        in_specs=prompt_specs + decode_specs,
        out_specs=[pl.BlockSpec((1, tq, 2 * HW), lambda bi, qi, ki, pt: (bi, qi, 0)),
                   pl.BlockSpec((1, t, 2 * HW),
                                lambda bi, qi, ki, pt: (step_of(bi, qi, ki) * sub // n_groups, 0, 0))],
        scratch_shapes=prompt_scratch + decode_scratch)
    return pl.pallas_call(
        functools.partial(_attend_kernel, tq=tq, sk=sk, cw=cw, pp=pp, t=t, sub=sub,
                          n_groups=n_groups, total=total),
        grid_spec=grid_spec,
        out_shape=[jax.ShapeDtypeStruct((b, s, 2 * HW), BF16),
                   jax.ShapeDtypeStruct((db, t, 2 * HW), F32)],
        name="attend",
        compiler_params=pltpu.CompilerParams(
            dimension_semantics=("arbitrary", "arbitrary", "arbitrary"),
            vmem_limit_bytes=VMEM_LIMIT),
    )(page_table, *prompt_args, *decode_args)


def _mix_ffn_kernel(x_ref, hd_ref, wout_ref, wup_ref, wdn_ref, gpm_ref, gpf_ref, gqf_ref, y_ref):
    mix = _dot(hd_ref[...].astype(BF16), wout_ref[...])
    x1 = x_ref[...] + _rms_f32(mix, gpm_ref[...])
    h = _rms_f32(x1, gpf_ref[...]).astype(BF16)
    u = jnp.maximum(_dot(h, wup_ref[...]), 0.0)
    f = _dot((u * u).astype(BF16), wdn_ref[...])
    y_ref[...] = x1 + _rms_f32(f, gqf_ref[...])


def _mix_ffn(x, heads, w_out, w_up, w_down, g_post_mix, g_pre_ffn, g_post_ffn, tm):
    n = x.shape[0]
    tm = min(tm, n)
    assert n % tm == 0
    row = lambda w: pl.BlockSpec((tm, w), lambda i: (i, 0))
    const = lambda a: pl.BlockSpec(a.shape, lambda i: (0,) * a.ndim, pipeline_mode=pl.Buffered(1))
    consts = [w_out, w_up, w_down, g_post_mix, g_pre_ffn, g_post_ffn]
    return pl.pallas_call(
        _mix_ffn_kernel,
        grid=(n // tm,),
        in_specs=[row(D_MODEL), row(2 * HW)] + [const(a) for a in consts],
        out_specs=row(D_MODEL),
        out_shape=jax.ShapeDtypeStruct((n, D_MODEL), F32),
        name="mix_ffn",
        compiler_params=pltpu.CompilerParams(
            dimension_semantics=("arbitrary",), vmem_limit_bytes=VMEM_LIMIT),
    )(x, heads, *consts)


def _prep_weights(w_in, w_q_b, w_kv_b):
    w_in_p = jnp.pad(w_in, ((0, 0), (0, IN_COLS_PAD - IN_COLS))).astype(BF16)
    w_va_t = w_in[:, C_VA:C_VA + HW].T.astype(BF16)
    w_qb_p = jnp.pad(w_q_b, ((0, 0), (0, 0), (0, LANE - NOPE_B - ROPE_B))).reshape(Q_LORA, HW)
    w_uk = w_kv_b[..., :NOPE_B]
    w_uv = w_kv_b[..., NOPE_B:]
    w_uk_p = jnp.pad(w_uk, ((0, 0), (0, 0), (0, LANE - NOPE_B))).reshape(KV_LORA, HW)
    w_uv_f = w_uv.reshape(KV_LORA, HW)
    eye_r = jnp.eye(ROPE_B, dtype=w_in.dtype)
    per_head = []
    for h in range(H):
        lat = jnp.concatenate([w_uk[:, h, :].T, jnp.zeros((LANE - NOPE_B, KV_LORA), w_in.dtype)], 0)
        rope = jnp.concatenate([jnp.zeros((NOPE_B, ROPE_B), w_in.dtype), eye_r,
                                jnp.zeros((LANE - NOPE_B - ROPE_B, ROPE_B), w_in.dtype)], 0)
        per_head.append(jnp.concatenate([lat, rope], axis=1))
    w_abs = jnp.concatenate(per_head, axis=0)
    return (w_in_p, w_qb_p.astype(BF16), w_uk_p.astype(BF16), w_va_t, w_uv_f.T.astype(BF16),
            w_abs.astype(BF16), w_uv_f.astype(BF16))


def kernel(x_prompt, x_sample, cache_diff_k, cache_diff_v, cache_mla_latent, cache_mla_krope,
           page_table, meta_tokens, g_pre_mix, g_post_mix, g_pre_ffn, g_post_ffn, w_in, g_q_a,
           w_q_b, g_kv_a, w_kv_b, lambda_q1, lambda_k1, lambda_q2, lambda_k2, g_sub, w_out,
           w_up, w_down):
    b, s, d = x_prompt.shape
    db, t, _ = x_sample.shape
    n_pool = cache_diff_k.shape[1]
    past = page_table.shape[1] * PAGE
    assert w_in.shape[0] == 1, "single-layer trunk"
    layer = 0

    w_in_p, w_qb_p, w_uk_p, w_va_t, w_uv_t, w_abs, w_uv_f = _prep_weights(
        w_in[layer], w_q_b[layer], w_kv_b[layer])
    proj_w = (g_pre_mix, w_in_p, g_q_a, w_qb_p, g_kv_a, w_uk_p, w_va_t, w_uv_t)
    lams = (lambda_q1, lambda_k1, lambda_q2, lambda_k2)

    tm = 512
    pm = _project(meta_tokens.astype(x_prompt.dtype), jnp.arange(N_META), N_META, 1, proj_w)
    meta_v = pm[3].reshape(N_META * H, DV)
    px = _project(x_prompt.reshape(b * s, d), N_META + jnp.arange(s), tm, b, proj_w, meta=(meta_v, pm[4]))
    tms = min(tm, db * t)
    ps = _project(x_sample.reshape(db * t, d), past + (jnp.arange(tms) % t), tms, 1, proj_w)

    caches = (jnp.transpose(cache_diff_k[layer], (0, 2, 3, 4, 1)).reshape(n_pool, HW, PAGE),
              cache_diff_v[layer].reshape(n_pool, PAGE * H, DV),
              cache_mla_latent[layer],
              jnp.transpose(cache_mla_krope[layer], (0, 2, 1)))
    heads_p, heads_s = _attend(px, pm, ps, caches, page_table, w_abs, w_uv_f, g_sub, lams,
                               b, s, db, t, tq=512, sk=512, cw=512, pp=8)

    ffn_w = (w_out[layer].astype(BF16), w_up[layer].astype(BF16), w_down[layer].astype(BF16),
             g_post_mix, g_pre_ffn, g_post_ffn)
    y_prompt = _mix_ffn(x_prompt.reshape(b * s, d), heads_p.reshape(b * s, 2 * HW), *ffn_w, tm=512)
    y_sample = _mix_ffn(x_sample.reshape(db * t, d), heads_s.reshape(db * t, 2 * HW), *ffn_w, tm=512)

    def with_meta(x_part, m_part, tail):
        m = jnp.broadcast_to(m_part[None], (b, N_META, m_part.shape[-1]))
        full = jnp.concatenate([m, x_part.reshape(b, s, -1)], axis=1)
        return full.reshape((1, b, s + N_META) + tail)

    k_shape, v_shape = (H, 2, DH_A), (H, DV)
    return (y_prompt.reshape(b, s, d), y_sample.reshape(db, t, d),
            with_meta(px[1], pm[1], k_shape), px[3].reshape((1, b, s + N_META) + v_shape),
            px[4][None], with_meta(px[5], pm[5], (ROPE_B,)),
            ps[1].reshape((1, db, t) + k_shape), ps[3].reshape((1, db, t) + v_shape),
            ps[4].reshape(1, db, t, KV_LORA), ps[5].reshape(1, db, t, ROPE_B))
```

```python
import functools
import math

import jax
import jax.numpy as jnp
import numpy as np
from jax import lax
from jax.experimental import pallas as pl
from jax.experimental.pallas import tpu as pltpu

D_MODEL = 1024
N_META = 16
H = 4
DH_A = 64
DV = 128
ROT_A = 16
Q_LORA = 384
KV_LORA = 256
NOPE_B = 64
ROPE_B = 32
D_FF = 4096
PAGE = 128
ROPE_THETA = 500000.0
RMS_EPS = 1e-6
NEG_INF = -1e30
SCALE_A = DH_A ** -0.5
SCALE_B = (NOPE_B + ROPE_B) ** -0.5
LAM_INIT = 0.8 - 0.6 * math.exp(-0.3 * 0)

LANE = 128
AHEAD = 2
SCORE_KINDS, VALUE_KINDS = (0, 2, 3), (1,)
CHAIN_PAGES = 4
ONES_ROWS = 16
HW = H * LANE
IN_COLS = 2208
IN_COLS_PAD = 2304
C_QA, C_KA, C_VA, C_QL, C_CKV, C_KR = 0, 512, 1024, 1536, 1920, 2176

VMEM_LIMIT = 48 * 1024 * 1024

F32 = jnp.float32
BF16 = jnp.bfloat16


def _rms_f32(x, g):
    return x * lax.rsqrt(jnp.mean(x * x, axis=-1, keepdims=True) + RMS_EPS) * g


def _rot(x, c, sm, sp, half):
    w = x.shape[-1]
    return x * c + pltpu.roll(x, w - half, 1) * sm + pltpu.roll(x, half, 1) * sp


def _dot(a, b):
    return jnp.dot(a, b, preferred_element_type=F32)


def _dot_nt(a, b):
    return lax.dot_general(a, b, (((1,), (1,)), ((), ())), preferred_element_type=F32)


def _project_kernel(x_ref, g_ref, win_ref, gq_ref, wqb_ref, gkv_ref, wuk_ref, wvat_ref, wuvt_ref,
                    ca_ref, sma_ref, spa_ref, cb_ref, smb_ref, spb_ref, *rest, meta_rows, n_batch):
    if meta_rows:
        mv_ref, mc_ref = rest[:2]
        qa_o, kaf_o, kab_o, vaf_o, ckv_o, kr_o, qm_o, km_o, vat_o, vbt_o, vbuf, cbuf, sem, msem = rest[2:]
    else:
        qa_o, kaf_o, kab_o, vaf_o, ckv_o, kr_o, qm_o, km_o, vat_o, vbt_o = rest
    h = _rms_f32(x_ref[...], g_ref[...]).astype(BF16)
    zl = _dot(h, win_ref[:, C_QL:])
    z = _dot(h, win_ref[:, :C_QL])

    ckv = _rms_f32(zl[:, C_CKV - C_QL:C_KR - C_QL], gkv_ref[...])
    ckv_b = ckv.astype(BF16)
    ql = _rms_f32(zl[:, :Q_LORA], gq_ref[...]).astype(BF16)
    kn = _dot(ckv_b, wuk_ref[...])
    qb = _dot(ql, wqb_ref[...])
    vat_o[0] = _dot_nt(wvat_ref[...], h).astype(BF16)
    vbt_o[0] = _dot_nt(wuvt_ref[...], ckv_b).astype(BF16)

    ca, sma, spa = ca_ref[...], sma_ref[...], spa_ref[...]
    for j in range(H):
        sl = slice(j * LANE, (j + 1) * LANE)
        q = _rot(z[:, C_QA + j * LANE:C_QA + (j + 1) * LANE], ca, sma, spa, ROT_A // 2)
        k = _rot(z[:, C_KA + j * LANE:C_KA + (j + 1) * LANE], ca, sma, spa, ROT_A // 2)
        qa_o[:, sl] = (q * SCALE_A).astype(BF16)
        if meta_rows:
            kaf_o[0, sl, :] = k.T
        else:
            kaf_o[:, sl] = k
        kab_o[:, sl] = k.astype(BF16)

    cb, smb, spb = cb_ref[...], smb_ref[...], spb_ref[...]
    kr = _rot(zl[:, C_KR - C_QL:], cb, smb, spb, ROPE_B // 2)
    kr_o[...] = kr[:, :ROPE_B]
    kr_at64 = pltpu.roll(kr, NOPE_B, 1)

    cq = pltpu.roll(cb, NOPE_B, 1)
    smq = pltpu.roll(smb, NOPE_B, 1)
    spq = pltpu.roll(spb, NOPE_B, 1)
    for j in range(H):
        sl = slice(j * LANE, (j + 1) * LANE)
        qm_o[:, sl] = _rot(qb[:, sl], cq, smq, spq, ROPE_B // 2).astype(BF16)
        km_o[:, sl] = (kn[:, sl] + kr_at64).astype(BF16)

    if meta_rows:
        i = pl.program_id(0)
        tm = x_ref.shape[0]
        bi, ti, slot = i % n_batch, i // n_batch, i % 2

        def row_copies(to_slot, b_idx, t_idx):
            return (pltpu.make_async_copy(
                        vbuf.at[to_slot], vaf_o.at[b_idx, pl.ds((meta_rows + t_idx * tm) * H, tm * H), :],
                        sem.at[to_slot, 0]),
                    pltpu.make_async_copy(
                        cbuf.at[to_slot], ckv_o.at[b_idx, pl.ds(meta_rows + t_idx * tm, tm), :],
                        sem.at[to_slot, 1]))

        def meta_copies(b_idx):
            return (pltpu.make_async_copy(mv_ref, vaf_o.at[b_idx, pl.ds(0, meta_rows * H), :], msem.at[0]),
                    pltpu.make_async_copy(mc_ref, ckv_o.at[b_idx, pl.ds(0, meta_rows), :], msem.at[1]))

        @pl.when(i >= 2)
        def _():
            for cp in row_copies(slot, 0, 0):
                cp.wait()

        for j in range(H):
            vbuf[slot, pl.ds(j, tm, stride=H), :] = z[:, C_VA + j * LANE:C_VA + (j + 1) * LANE]
        cbuf[slot] = ckv
        for cp in row_copies(slot, bi, ti):
            cp.start()

        @pl.when(ti == 0)
        def _():
            for cp in meta_copies(bi):
                cp.start()

        @pl.when(ti == 1)
        def _():
            for cp in meta_copies(bi):
                cp.wait()

        @pl.when(i == pl.num_programs(0) - 1)
        def _():
            for cp in row_copies(1 - slot, 0, 0) + row_copies(slot, 0, 0):
                cp.wait()
    else:
        vaf_o[...] = z[:, C_VA:C_VA + HW]
        ckv_o[...] = ckv


def _rope_tables(pos):
    pos = pos.astype(F32)[:, None]
    lane = np.arange(LANE)

    def build(period, half, active):
        idx = lane % period
        inv = 1.0 / (ROPE_THETA ** (jnp.arange(half, dtype=F32) / half))
        ang = pos * inv[None, :]
        cos = jnp.cos(ang)[:, idx % half]
        sin = jnp.sin(ang)[:, idx % half]
        first = jnp.asarray(active & (idx < half))
        second = jnp.asarray(active & (idx >= half) & (idx < 2 * half))
        c = jnp.where(first | second, cos, 1.0)
        sm = jnp.where(first, -sin, 0.0)
        sp = jnp.where(second, sin, 0.0)
        return c, sm, sp

    a = build(DH_A, ROT_A // 2, np.ones(LANE, bool))
    b = build(LANE, ROPE_B // 2, lane < ROPE_B)
    return a + b


def _project(x, pos, tm, nb, wts, meta=None):
    n = x.shape[0]
    n_pos_tiles = pos.shape[0] // tm
    tiles_per_b = n // nb // tm
    tables = _rope_tables(pos)
    grid = (n // tm,)
    row = lambda w: pl.BlockSpec((tm, w), lambda i: ((i % nb) * tiles_per_b + i // nb, 0))
    full = lambda a: pl.BlockSpec(a.shape, lambda i: (0,) * a.ndim)
    tab = pl.BlockSpec((tm, LANE), lambda i: ((i // nb) % n_pos_tiles, 0))
    colmajor = pl.BlockSpec((1, HW, tm), lambda i: (i % nb, 0, i // nb))
    outs = [(HW, BF16), (HW, F32), (HW, BF16), (HW, F32), (KV_LORA, F32), (ROPE_B, F32),
            (HW, BF16), (HW, BF16)]
    out_specs = [row(w) for w, _ in outs] + [colmajor, colmajor]
    t_shape = jax.ShapeDtypeStruct((nb, HW, n // nb), BF16)
    out_shape = [jax.ShapeDtypeStruct((n, w), dt) for w, dt in outs] + [t_shape, t_shape]
    extra_in, extra_specs, scratch, meta_rows = [], [], [], 0
    if meta is not None:
        assert tiles_per_b >= 2
        meta_rows = meta[1].shape[0]
        rows = meta_rows + n // nb
        extra_in, extra_specs = list(meta), [full(a) for a in meta]
        out_specs[1] = colmajor
        out_shape[1] = jax.ShapeDtypeStruct((nb, HW, n // nb), F32)
        out_specs[3] = out_specs[4] = pl.BlockSpec(memory_space=pl.ANY)
        out_shape[3] = jax.ShapeDtypeStruct((nb, rows * H, DV), F32)
        out_shape[4] = jax.ShapeDtypeStruct((nb, rows, KV_LORA), F32)
        scratch = [pltpu.VMEM((2, tm * H, DV), F32), pltpu.VMEM((2, tm, KV_LORA), F32),
                   pltpu.SemaphoreType.DMA((2, 2)), pltpu.SemaphoreType.DMA((2,))]
    return pl.pallas_call(
        functools.partial(_project_kernel, meta_rows=meta_rows, n_batch=nb),
        grid=grid,
        in_specs=[row(D_MODEL)] + [full(a) for a in wts] + [tab] * 6 + extra_specs,
        out_specs=out_specs,
        out_shape=out_shape,
        scratch_shapes=scratch,
        name="project",
        compiler_params=pltpu.CompilerParams(
            dimension_semantics=("arbitrary",), vmem_limit_bytes=VMEM_LIMIT),
    )(x, *wts, *tables, *extra_in)


def _lambda(lq1, lk1, lq2, lk2):
    return (jnp.exp(jnp.sum(lq1[...] * lk1[...], keepdims=True))
            - jnp.exp(jnp.sum(lq2[...] * lk2[...], keepdims=True)) + LAM_INIT)


def _prompt_step(qi, ki, qa_ref, qm_ref, ka_ref, km_ref, vat_ref, vbt_ref,
                 mka_ref, mkm_ref, mvat_ref, mvbt_ref, gcol_ref, lq1, lk1, lq2, lk2,
                 out_ref, qw, ma, acca, mb, accb, *, tq, sk, cw):
    def update(ka, km, vat, vbt, mask):
        streams = [(h, grp, cb) for h in range(H)
                   for grp, ncb in (("a", 2 * tq // cw), ("b", tq // cw)) for cb in range(ncb)]

        def scores(st):
            h, grp, cb = st
            sl = slice(h * LANE, (h + 1) * LANE)
            cols = slice(cb * cw, (cb + 1) * cw)
            if grp == "a":
                return _dot_nt(ka[:, sl], qw[h, cols, :])
            return _dot_nt(km[:, sl], qm_ref[0, cols, sl]) * SCALE_B

        pending = [scores(st) for st in streams[:AHEAD]]
        for i, (h, grp, cb) in enumerate(streams):
            s = pending.pop(0)
            if i + AHEAD < len(streams):
                pending.append(scores(streams[i + AHEAD]))
            sl = slice(h * LANE, (h + 1) * LANE)
            idx = (h, slice(None), slice(cb * cw, (cb + 1) * cw))
            qcol = (cb * cw) % tq
            mk = None if mask is None else mask[:, qcol:qcol + cw]
            m_ref, acc_ref, vt = (ma, acca, vat) if grp == "a" else (mb, accb, vbt)
            if mk is not None:
                s = jnp.where(mk, s, NEG_INF)
            m_old = m_ref[idx]
            m_new = jnp.maximum(m_old, jnp.max(s, axis=0, keepdims=True))
            p = jnp.exp((s - m_new).astype(BF16))
            if mk is not None:
                p = jnp.where(mk, p, 0)
            m_ref[idx] = m_new
            v_ext = jnp.concatenate([vt[sl, :], jnp.ones((ONES_ROWS, vt.shape[1]), BF16)], axis=0)
            acc_ref[idx] = acc_ref[idx] * jnp.exp(m_old - m_new) + _dot(v_ext, p)

    @pl.when(ki == 0)
    def _():
        lane = lax.broadcasted_iota(jnp.int32, (tq, LANE), 1)
        for h in range(H):
            q = qa_ref[0, :, h * LANE:(h + 1) * LANE]
            qw[h, :tq] = jnp.where(lane < DH_A, q, 0)
            qw[h, tq:] = jnp.where(lane >= DH_A, q, 0)
        ma[...] = jnp.full(ma.shape, NEG_INF, F32)
        mb[...] = jnp.full(mb.shape, NEG_INF, F32)
        acca[...] = jnp.zeros(acca.shape, F32)
        accb[...] = jnp.zeros(accb.shape, F32)
        update(mka_ref[...], mkm_ref[...], mvat_ref[0], mvbt_ref[0], None)

    @pl.when(ki < qi)
    def _():
        for j in range(tq // sk):
            rows = slice(j * sk, (j + 1) * sk)
            update(ka_ref[0, rows, :], km_ref[0, rows, :], vat_ref[0, :, rows], vbt_ref[0, :, rows],
                   None)

    @pl.when(ki == qi)
    def _():
        r = lax.broadcasted_iota(jnp.int32, (sk, tq), 0)
        c = lax.broadcasted_iota(jnp.int32, (sk, tq), 1)
        for j in range(tq // sk):
            rows = slice(j * sk, (j + 1) * sk)
            update(ka_ref[0, rows, :], km_ref[0, rows, :], vat_ref[0, :, rows], vbt_ref[0, :, rows],
                   r + j * sk <= c)
        lam = _lambda(lq1, lk1, lq2, lk2)
        for h in range(H):
            o12 = acca[h, :DV] / acca[h, DV:DV + 1]
            d = o12[:, :tq] - lam * o12[:, tq:]
            ms = jnp.mean(d * d, axis=0, keepdims=True)
            o = d * lax.rsqrt(ms + RMS_EPS) * gcol_ref[...] * (1.0 - LAM_INIT)
            out_ref[0, :, h * LANE:(h + 1) * LANE] = o.T.astype(out_ref.dtype)
            ob = accb[h, :DV] / accb[h, DV:DV + 1]
            out_ref[0, :, HW + h * LANE:HW + (h + 1) * LANE] = ob.T.astype(out_ref.dtype)


def _decode_steps(first, pt_ref, qa_ref, qm_ref, kan_ref, van_ref, cn_ref, krn_ref,
                  wabs_ref, wuv_ref, gsub_ref, lq1, lk1, lq2, lk2,
                  kt_hbm, v_hbm, c_hbm, krt_hbm,
                  out_ref, kt_buf, v_buf, c_buf, krt_buf, sem,
                  qbd, qext, ma, la, acca, mb, lb, accb, *, pp, t, sub, n_groups, total):
    last = first + sub - 1
    streams = ((kt_hbm, kt_buf), (v_hbm, v_buf), (c_hbm, c_buf), (krt_hbm, krt_buf))

    def page_copy(kind, page, to_slot, p):
        hbm, buf = streams[kind]
        return pltpu.make_async_copy(hbm.at[page], buf.at[to_slot, p], sem.at[to_slot, kind])

    def fetch(target, to_slot, kinds):
        seq, grp = target // n_groups, target % n_groups
        for p in range(pp):
            page = pt_ref[seq, grp * pp + p]
            for kind in kinds:
                page_copy(kind, page, to_slot, p).start()

    def drain(of_slot):
        for p in range(pp):
            for kind in range(len(streams)):
                page_copy(kind, 0, of_slot, p).wait()

    @pl.when(first == 0)
    def _():
        for slot in range(sub):
            fetch(slot, slot, range(len(streams)))

    ra, rb = 2 * H * t, H * t
    head_a = (lax.broadcasted_iota(jnp.int32, (ra, LANE), 0) // t) % H

    def rep_rows(q, rows):
        tok = lax.broadcasted_iota(jnp.int32, (rows, HW), 0) % t
        out = jnp.broadcast_to(q[t - 1:t], (rows, HW))
        for i in range(t - 2, -1, -1):
            out = jnp.where(tok == i, jnp.broadcast_to(q[i:i + 1], (rows, HW)), out)
        return out

    @pl.when(first % n_groups == 0)
    def _():
        r = lax.broadcasted_iota(jnp.int32, (ra, HW), 0)
        ln = lax.broadcasted_iota(jnp.int32, (ra, HW), 1)
        chunk = ((r // t) % H) * 2 + r // (H * t)
        qbd[...] = jnp.where(ln // DH_A == chunk, rep_rows(qa_ref[0], ra), 0.0).astype(BF16)
        r = lax.broadcasted_iota(jnp.int32, (rb, HW), 0)
        ln = lax.broadcasted_iota(jnp.int32, (rb, HW), 1)
        q = jnp.where(ln // LANE == r // t, rep_rows(qm_ref[0], rb), 0.0).astype(BF16)
        qext[...] = _dot(q, wabs_ref[...]).astype(BF16)
        ma[...] = jnp.full(ma.shape, NEG_INF, F32)
        mb[...] = jnp.full(mb.shape, NEG_INF, F32)
        la[...] = jnp.zeros(la.shape, F32)
        lb[...] = jnp.zeros(lb.shape, F32)
        acca[...] = jnp.zeros(acca.shape, F32)
        accb[...] = jnp.zeros(accb.shape, F32)

    def pv_a(p, v_of_head):
        p = p.astype(BF16)
        new = jnp.zeros((ra, DV), F32)
        for h in range(0, H, 2):
            pv = _dot(p, jnp.concatenate([v_of_head(h), v_of_head(h + 1)], axis=1))
            new = jnp.where(head_a == h, pv[:, :DV], jnp.where(head_a == h + 1, pv[:, DV:], new))
        return new

    def advance(state, s, mask, pv):
        m_old, l_old, acc = state
        if mask is not None:
            s = jnp.where(mask, s, NEG_INF)
        m_new = jnp.maximum(m_old, jnp.max(s, axis=1, keepdims=True))
        p = jnp.exp(s - m_new)
        if mask is not None:
            p = jnp.where(mask, p, 0.0)
        corr = jnp.exp(m_old - m_new)
        return m_new, l_old * corr + jnp.sum(p, axis=1, keepdims=True), acc * corr + pv(p)

    qe = qext[...]
    st_a = (ma[0], la[0], acca[0])
    st_b = (mb[0], lb[0], accb[0])
    for slot in range(sub):
        drain(slot)
    chains = []
    for slot in range(sub):
        for first_page in range(0, pp, CHAIN_PAGES):
            pages = range(first_page, first_page + CHAIN_PAGES)
            kt = jnp.concatenate([kt_buf[slot, p].astype(BF16) for p in pages], axis=1)
            krt = jnp.concatenate([krt_buf[slot, p].astype(BF16) for p in pages], axis=1)
            c = jnp.concatenate([c_buf[slot, p].astype(BF16) for p in pages], axis=0)
            s_a = _dot(qbd[...], kt)
            s_b = (_dot_nt(qe[:, :KV_LORA], c) + _dot(qe[:, KV_LORA:], krt)) * SCALE_B
            chains.append((slot, pages, s_a, s_b, c))
        fetch((first + sub + slot) % total, slot, SCORE_KINDS)
    for slot, pages, s_a, s_b, c in chains:
        def page_values(h, slot=slot, pages=pages):
            return jnp.concatenate([v_buf[slot, p, pl.ds(h, PAGE, stride=H), :].astype(BF16)
                                    for p in pages], axis=0)

        st_a = advance(st_a, s_a, None, lambda p: pv_a(p, page_values))
        st_b = advance(st_b, s_b, None, lambda p, c=c: _dot(p.astype(BF16), c))
        if pages[-1] == pp - 1:
            fetch((first + sub + slot) % total, slot, VALUE_KINDS)
    ma[0], la[0], acca[0] = st_a
    mb[0], lb[0], accb[0] = st_b

    @pl.when(last % n_groups == n_groups - 1)
    def _():
        nk = 16

        def pad_rows(x):
            r = lax.broadcasted_iota(jnp.int32, (nk, x.shape[1]), 0)
            out = jnp.zeros((nk, x.shape[1]), x.dtype)
            for i in range(t):
                out = jnp.where(r == i, jnp.broadcast_to(x[i:i + 1], out.shape), out)
            return out.astype(BF16)

        kn, vn, cn, krn = (pad_rows(r[0]) for r in (kan_ref, van_ref, cn_ref, krn_ref))
        causal_a = (lax.broadcasted_iota(jnp.int32, (ra, nk), 1)
                    <= lax.broadcasted_iota(jnp.int32, (ra, nk), 0) % t)
        causal_b = (lax.broadcasted_iota(jnp.int32, (rb, nk), 1)
                    <= lax.broadcasted_iota(jnp.int32, (rb, nk), 0) % t)
        s_n = (_dot_nt(qe[:, :KV_LORA], cn) + _dot_nt(qe[:, KV_LORA:], krn)) * SCALE_B
        _, l_a, acc_a = advance(st_a, _dot_nt(qbd[...], kn), causal_a,
                                lambda p: pv_a(p, lambda h: vn[:, h * LANE:(h + 1) * LANE]))
        _, l_b, acc_b = advance(st_b, s_n, causal_b, lambda p: _dot(p.astype(BF16), cn))

        lam = _lambda(lq1, lk1, lq2, lk2)
        oa = acc_a / l_a
        o_a = _rms_f32(oa[:rb] - lam * oa[rb:], gsub_ref[...]) * (1.0 - LAM_INIT)
        o_b = _dot((acc_b / l_b).astype(BF16), wuv_ref[...])
        for h in range(H):
            rows = slice(h * t, (h + 1) * t)
            out_ref[0, :, h * LANE:(h + 1) * LANE] = o_a[rows, :]
            out_ref[0, :, HW + h * LANE:HW + (h + 1) * LANE] = o_b[rows, h * LANE:(h + 1) * LANE]

    @pl.when(last == total - 1)
    def _():
        for slot in range(sub):
            drain(slot)


N_PROMPT_IN, N_PROMPT_SCRATCH = 15, 5
N_DECODE_IN, N_DECODE_SCRATCH = 17, 13


def _attend_kernel(pt_ref, *refs, tq, sk, cw, pp, t, sub, n_groups, total):
    prompt_in, refs = refs[:N_PROMPT_IN], refs[N_PROMPT_IN:]
    decode_in, refs = refs[:N_DECODE_IN], refs[N_DECODE_IN:]
    (out_p, out_s), refs = refs[:2], refs[2:]
    prompt_scr, decode_scr = refs[:N_PROMPT_SCRATCH], refs[N_PROMPT_SCRATCH:]
    assert len(decode_scr) == N_DECODE_SCRATCH
    qi, ki = pl.program_id(1), pl.program_id(2)
    step = (pl.program_id(0) * pl.num_programs(1) + qi) * pl.num_programs(2) + ki
    _decode_steps(step * sub, pt_ref, *decode_in, out_s, *decode_scr,
                  pp=pp, t=t, sub=sub, n_groups=n_groups, total=total)
    _prompt_step(qi, ki, *prompt_in, out_p, *prompt_scr, tq=tq, sk=sk, cw=cw)


def _attend(px, pm, ps, caches, page_table, wabs, wuv, g_sub, lams, b, s, db, t, tq, sk, cw, pp):
    qa, _, kab, _, _, _, qm, km, vat, vbt = px
    _, _, mkab, _, _, _, _, mkm, mvat, mvbt = pm
    sqa, skaf, _, svaf, sckv, skr, sqm, _, _, _ = ps
    nq = s // tq
    n_grid = b * nq * nq
    n_groups = page_table.shape[1] // pp
    total = db * n_groups
    sub = total // n_grid
    assert sub * n_grid == total and n_groups % sub == 0
    step_of = lambda bi, qi, ki: (bi * nq + qi) * nq + ki

    r3 = lambda a: a.reshape(b, s, HW)
    qspec = pl.BlockSpec((1, tq, HW), lambda bi, qi, ki, pt: (bi, qi, 0))
    kspec = pl.BlockSpec((1, tq, HW), lambda bi, qi, ki, pt: (bi, jnp.minimum(ki, qi), 0))
    vspec = pl.BlockSpec((1, HW, tq), lambda bi, qi, ki, pt: (bi, 0, jnp.minimum(ki, qi)))
    full = lambda a: pl.BlockSpec(a.shape, lambda bi, qi, ki, pt: (0,) * a.ndim)
    seq3 = lambda a: a.reshape(db, t, a.shape[-1])
    tok = lambda w: pl.BlockSpec((1, t, w),
                                 lambda bi, qi, ki, pt: (step_of(bi, qi, ki) * sub // n_groups, 0, 0))
    gcol = jnp.broadcast_to(g_sub.reshape(DV, 1), (DV, tq))
    prompt_args = [r3(qa), r3(qm), r3(kab), r3(km), vat, vbt, mkab, mkm, mvat, mvbt, gcol] + list(lams)
    prompt_specs = [qspec, qspec, kspec, kspec, vspec, vspec] + [full(a) for a in prompt_args[6:]]
    decode_consts = [wabs, wuv, g_sub] + list(lams)
    decode_args = ([seq3(sqa.astype(F32)), seq3(sqm.astype(F32)), seq3(skaf), seq3(svaf), seq3(sckv),
                    seq3(skr)] + decode_consts + list(caches))
    decode_specs = ([tok(HW), tok(HW), tok(HW), tok(HW), tok(KV_LORA), tok(ROPE_B)]
                    + [full(a) for a in decode_consts]
                    + [pl.BlockSpec(memory_space=pl.ANY)] * len(caches))
    assert len(prompt_args) == N_PROMPT_IN and len(decode_args) == N_DECODE_IN
    ra, rb = 2 * t * H, t * H
    prompt_scratch = [pltpu.VMEM((H, 2 * tq, LANE), BF16),
                      pltpu.VMEM((H, 1, 2 * tq), F32), pltpu.VMEM((H, DV + ONES_ROWS, 2 * tq), F32),
                      pltpu.VMEM((H, 1, tq), F32), pltpu.VMEM((H, DV + ONES_ROWS, tq), F32)]
    decode_scratch = ([pltpu.VMEM((sub, pp) + a.shape[1:], a.dtype) for a in caches]
                      + [pltpu.SemaphoreType.DMA((sub, len(caches))),
                         pltpu.VMEM((ra, HW), BF16), pltpu.VMEM((rb, KV_LORA + ROPE_B), BF16),
                         pltpu.VMEM((1, ra, 1), F32), pltpu.VMEM((1, ra, 1), F32),
                         pltpu.VMEM((1, ra, DV), F32),
                         pltpu.VMEM((1, rb, 1), F32), pltpu.VMEM((1, rb, 1), F32),
                         pltpu.VMEM((1, rb, KV_LORA), F32)])
    assert len(prompt_scratch) == N_PROMPT_SCRATCH and len(decode_scratch) == N_DECODE_SCRATCH
    grid_spec = pltpu.PrefetchScalarGridSpec(
        num_scalar_prefetch=1,
        grid=(b, nq, nq),
        in_specs=prompt_specs + decode_specs,
        out_specs=[pl.BlockSpec((1, tq, 2 * HW), lambda bi, qi, ki, pt: (bi, qi, 0)),
                   pl.BlockSpec((1, t, 2 * HW),
                                lambda bi, qi, ki, pt: (step_of(bi, qi, ki) * sub // n_groups, 0, 0))],
        scratch_shapes=prompt_scratch + decode_scratch)
    return pl.pallas_call(
        functools.partial(_attend_kernel, tq=tq, sk=sk, cw=cw, pp=pp, t=t, sub=sub,
                          n_groups=n_groups, total=total),
        grid_spec=grid_spec,
        out_shape=[jax.ShapeDtypeStruct((b, s, 2 * HW), BF16),
                   jax.ShapeDtypeStruct((db, t, 2 * HW), F32)],
        name="attend",
        compiler_params=pltpu.CompilerParams(
            dimension_semantics=("arbitrary", "arbitrary", "arbitrary"),
            vmem_limit_bytes=VMEM_LIMIT),
    )(page_table, *prompt_args, *decode_args)


def _mix_ffn_kernel(x_ref, hd_ref, wout_ref, wup_ref, wdn_ref, gpm_ref, gpf_ref, gqf_ref, y_ref):
    mix = _dot(hd_ref[...].astype(BF16), wout_ref[...])
    x1 = x_ref[...] + _rms_f32(mix, gpm_ref[...])
    h = _rms_f32(x1, gpf_ref[...]).astype(BF16)
    u = jnp.maximum(_dot(h, wup_ref[...]), 0.0)
    f = _dot((u * u).astype(BF16), wdn_ref[...])
    y_ref[...] = x1 + _rms_f32(f, gqf_ref[...])


def _mix_ffn(x, heads, w_out, w_up, w_down, g_post_mix, g_pre_ffn, g_post_ffn, tm):
    n = x.shape[0]
    tm = min(tm, n)
    assert n % tm == 0
    row = lambda w: pl.BlockSpec((tm, w), lambda i: (i, 0))
    const = lambda a: pl.BlockSpec(a.shape, lambda i: (0,) * a.ndim, pipeline_mode=pl.Buffered(1))
    consts = [w_out, w_up, w_down, g_post_mix, g_pre_ffn, g_post_ffn]
    return pl.pallas_call(
        _mix_ffn_kernel,
        grid=(n // tm,),
        in_specs=[row(D_MODEL), row(2 * HW)] + [const(a) for a in consts],
        out_specs=row(D_MODEL),
        out_shape=jax.ShapeDtypeStruct((n, D_MODEL), F32),
        name="mix_ffn",
        compiler_params=pltpu.CompilerParams(
            dimension_semantics=("arbitrary",), vmem_limit_bytes=VMEM_LIMIT),
    )(x, heads, *consts)


def _prep_weights(w_in, w_q_b, w_kv_b):
    w_in_p = jnp.pad(w_in, ((0, 0), (0, IN_COLS_PAD - IN_COLS))).astype(BF16)
    w_va_t = w_in[:, C_VA:C_VA + HW].T.astype(BF16)
    w_qb_p = jnp.pad(w_q_b, ((0, 0), (0, 0), (0, LANE - NOPE_B - ROPE_B))).reshape(Q_LORA, HW)
    w_uk = w_kv_b[..., :NOPE_B]
    w_uv = w_kv_b[..., NOPE_B:]
    w_uk_p = jnp.pad(w_uk, ((0, 0), (0, 0), (0, LANE - NOPE_B))).reshape(KV_LORA, HW)
    w_uv_f = w_uv.reshape(KV_LORA, HW)
    eye_r = jnp.eye(ROPE_B, dtype=w_in.dtype)
    per_head = []
    for h in range(H):
        lat = jnp.concatenate([w_uk[:, h, :].T, jnp.zeros((LANE - NOPE_B, KV_LORA), w_in.dtype)], 0)
        rope = jnp.concatenate([jnp.zeros((NOPE_B, ROPE_B), w_in.dtype), eye_r,
                                jnp.zeros((LANE - NOPE_B - ROPE_B, ROPE_B), w_in.dtype)], 0)
        per_head.append(jnp.concatenate([lat, rope], axis=1))
    w_abs = jnp.concatenate(per_head, axis=0)
    return (w_in_p, w_qb_p.astype(BF16), w_uk_p.astype(BF16), w_va_t, w_uv_f.T.astype(BF16),
            w_abs.astype(BF16), w_uv_f.astype(BF16))


def kernel(x_prompt, x_sample, cache_diff_k, cache_diff_v, cache_mla_latent, cache_mla_krope,
           page_table, meta_tokens, g_pre_mix, g_post_mix, g_pre_ffn, g_post_ffn, w_in, g_q_a,
           w_q_b, g_kv_a, w_kv_b, lambda_q1, lambda_k1, lambda_q2, lambda_k2, g_sub, w_out,
           w_up, w_down):
    b, s, d = x_prompt.shape
    db, t, _ = x_sample.shape
    n_pool = cache_diff_k.shape[1]
    past = page_table.shape[1] * PAGE
    assert w_in.shape[0] == 1, "single-layer trunk"
    layer = 0

    w_in_p, w_qb_p, w_uk_p, w_va_t, w_uv_t, w_abs, w_uv_f = _prep_weights(
        w_in[layer], w_q_b[layer], w_kv_b[layer])
    proj_w = (g_pre_mix, w_in_p, g_q_a, w_qb_p, g_kv_a, w_uk_p, w_va_t, w_uv_t)
    lams = (lambda_q1, lambda_k1, lambda_q2, lambda_k2)

    tm = 512
    pm = _project(meta_tokens.astype(x_prompt.dtype), jnp.arange(N_META), N_META, 1, proj_w)
    meta_v = pm[3].reshape(N_META * H, DV)
    px = _project(x_prompt.reshape(b * s, d), N_META + jnp.arange(s), tm, b, proj_w, meta=(meta_v, pm[4]))
    tms = min(tm, db * t)
    ps = _project(x_sample.reshape(db * t, d), past + (jnp.arange(tms) % t), tms, 1, proj_w)

    caches = (jnp.transpose(cache_diff_k[layer], (0, 2, 3, 4, 1)).reshape(n_pool, HW, PAGE),
              cache_diff_v[layer].reshape(n_pool, PAGE * H, DV),
              cache_mla_latent[layer],
              jnp.transpose(cache_mla_krope[layer], (0, 2, 1)))
    heads_p, heads_s = _attend(px, pm, ps, caches, page_table, w_abs, w_uv_f, g_sub, lams,
                               b, s, db, t, tq=512, sk=512, cw=512, pp=8)

    ffn_w = (w_out[layer].astype(BF16), w_up[layer].astype(BF16), w_down[layer].astype(BF16),
             g_post_mix, g_pre_ffn, g_post_ffn)
    y_prompt = _mix_ffn(x_prompt.reshape(b * s, d), heads_p.reshape(b * s, 2 * HW), *ffn_w, tm=512)
    y_sample = _mix_ffn(x_sample.reshape(db * t, d), heads_s.reshape(db * t, 2 * HW), *ffn_w, tm=512)

    def with_meta(x_part, m_part, tail):
        m = jnp.broadcast_to(m_part[None], (b, N_META, m_part.shape[-1]))
        full = jnp.concatenate([m, x_part.reshape(b, s, -1)], axis=1)
        return full.reshape((1, b, s + N_META) + tail)

    k_shape, v_shape = (H, 2, DH_A), (H, DV)
    meta_kt = jnp.broadcast_to(pm[1].T[None], (b, HW, N_META))
    p_diff_k = jnp.concatenate([meta_kt, px[1]], axis=2).reshape((b,) + k_shape + (s + N_META,))
    p_diff_k = jnp.transpose(p_diff_k, (0, 4, 1, 2, 3))[None]
    return (y_prompt.reshape(b, s, d), y_sample.reshape(db, t, d),
            p_diff_k, px[3].reshape((1, b, s + N_META) + v_shape),
            px[4][None], with_meta(px[5], pm[5], (ROPE_B,)),
            ps[1].reshape((1, db, t) + k_shape), ps[3].reshape((1, db, t) + v_shape),
            ps[4].reshape(1, db, t, KV_LORA), ps[5].reshape(1, db, t, ROPE_B))
```
